```python
import jax
import jax.numpy as jnp
from jax import lax
import numpy as np

D_MODEL = 2048
BATCH = 1
SEQ = 8192
DEPTH = 4

ATT_HEADS = 8
ATT_HEAD_DIM = 128
IDX_HEADS = 8
IDX_DIM = 64
INDEX_TOPK = 256
Q_BLOCK = 128
MLSTM_HEADS = 4
MLSTM_QK_DIM = 128
MLSTM_V_DIM = 256
MLSTM_CHUNK = 64
CONV_WIDTH = 4
RET_HEADS = 8
RET_QK_DIM = 128
RET_V_DIM = 128
RET_CHUNK = 128
ROPE_THETA = 10000.0
NORM_EPS = 1e-6
FFN_DIM = -(-8 * D_MODEL // (3 * 256)) * 256
N_BRANCHES = 3

SPLIT_SIZES = (
    ATT_HEADS * ATT_HEAD_DIM,
    ATT_HEADS * ATT_HEAD_DIM,
    ATT_HEADS * ATT_HEAD_DIM,
    IDX_HEADS * IDX_DIM,
    IDX_DIM,
    IDX_HEADS,
    MLSTM_HEADS * MLSTM_QK_DIM,
    MLSTM_HEADS * MLSTM_QK_DIM,
    MLSTM_HEADS * MLSTM_V_DIM,
    MLSTM_HEADS,
    MLSTM_HEADS,
    MLSTM_HEADS * MLSTM_V_DIM,
    RET_HEADS * RET_QK_DIM,
    RET_HEADS * RET_QK_DIM,
    RET_HEADS * RET_V_DIM,
    RET_HEADS * RET_V_DIM,
    N_BRANCHES * D_MODEL,
)
PROJ_DIM = sum(SPLIT_SIZES)

kernel_name = 'hybrid_dsa_mlstm_retention_trunk'


def rms_norm(x, g):
    xf = x.astype(jnp.float32)
    y = xf * lax.rsqrt(jnp.mean(xf * xf, axis=-1, keepdims=True) + NORM_EPS)
    return (y * g).astype(x.dtype)


def head_layer_norm(x, g):
    xf = x.astype(jnp.float32)
    mu = jnp.mean(xf, axis=-1, keepdims=True)
    xc = xf - mu
    y = xc * lax.rsqrt(jnp.mean(xc * xc, axis=-1, keepdims=True) + NORM_EPS)
    return (y * g).astype(x.dtype)


def split_heads(t, n):
    return t.reshape(t.shape[:-1] + (n, -1))


def rope(x, pos):
    d = x.shape[-1]
    half = d // 2
    inv_freq = ROPE_THETA ** (-jnp.arange(half, dtype=jnp.float32) * 2.0 / d)
    ang = pos.astype(jnp.float32)[:, None] * inv_freq[None, :]
    cos = jnp.cos(ang)[:, None, :]
    sin = jnp.sin(ang)[:, None, :]
    xf = x.astype(jnp.float32)
    x1, x2 = xf[..., :half], xf[..., half:]
    return jnp.concatenate([x1 * cos - x2 * sin, x2 * cos + x1 * sin], axis=-1).astype(x.dtype)


def causal_depthwise_conv(x, w):
    c = x.shape[-1]
    return lax.conv_general_dilated(
        x, w[:, None, :].astype(x.dtype), window_strides=(1,),
        padding=[(w.shape[0] - 1, 0)], dimension_numbers=('NWC', 'WIO', 'NWC'),
        feature_group_count=c)


def to_chunks(a, size):
    b, s, h, d = a.shape
    return jnp.moveaxis(a.astype(jnp.float32).reshape(b, s // size, size, h, d), (1, 3), (0, 2))


def from_chunks(a):
    nc, b, h, size, d = a.shape
    return jnp.moveaxis(a, (0, 2), (1, 3)).reshape(b, nc * size, h, d)


def dsa_sparse_attention(q, k, v, iq, ik, iw):
    b, s, h, dh = q.shape
    n_sel = min(INDEX_TOPK, s // 4)
    nb = s // Q_BLOCK
    key_pos = jnp.arange(s, dtype=jnp.int32)

    def to_blocks(a):
        return jnp.moveaxis(a.reshape((b, nb, Q_BLOCK) + a.shape[2:]), 1, 0)

    def block(args):
        qb, iqb, iwb, start = args
        q_pos = start + jnp.arange(Q_BLOCK, dtype=jnp.int32)
        causal = key_pos[None, :] <= q_pos[:, None]
        idx_logits = jnp.einsum('bqjd,bsd->bqjs', iqb, ik).astype(jnp.float32) * IDX_DIM ** -0.5
        w = iwb.astype(jnp.float32) * IDX_HEADS ** -0.5
        score = jnp.einsum('bqj,bqjs->bqs', w, jax.nn.relu(idx_logits))
        score = jnp.where(causal[None], score, -jnp.inf)
        _, sel = lax.top_k(score, n_sel)
        valid = sel <= q_pos[None, :, None]
        k_sel = jax.vmap(lambda kk, ii: kk[ii])(k, sel)
        v_sel = jax.vmap(lambda vv, ii: vv[ii])(v, sel)
        logits = jnp.einsum('bqhd,bqnhd->bhqn', qb, k_sel).astype(jnp.float32) * dh ** -0.5
        logits = jnp.where(valid[:, None], logits, -jnp.inf)
        p = jax.nn.softmax(logits, axis=-1)
        return jnp.einsum('bhqn,bqnhd->bqhd', p.astype(v.dtype), v_sel)

    starts = jnp.arange(nb, dtype=jnp.int32) * Q_BLOCK
    out = lax.map(block, (to_blocks(q), to_blocks(iq), to_blocks(iw), starts))
    return jnp.moveaxis(out, 0, 1).reshape(b, s, h * dh)


def mlstm_chunkwise(q, k, v, i_pre, f_pre):
    b, s, h, dk = q.shape
    dv = v.shape[-1]
    size = MLSTM_CHUNK
    qc = to_chunks(q, size)
    kc = to_chunks(k, size) * dk ** -0.5
    vc = to_chunks(v, size)
    ic = to_chunks(i_pre[..., None], size)[..., 0]
    logf = to_chunks(jax.nn.log_sigmoid(f_pre.astype(jnp.float32))[..., None], size)[..., 0]
    bc = jnp.cumsum(logf, axis=-1)
    tril = jnp.tril(jnp.ones((size, size), dtype=bool))

    def step(carry, xs):
        c_state, n_state, m_state = carry
        q_, k_, v_, b_, i_ = xs
        log_intra = jnp.where(tril, b_[..., :, None] - b_[..., None, :] + i_[..., None, :], -jnp.inf)
        m_inter = b_ + m_state[..., None]
        m_t = jnp.maximum(m_inter, jnp.max(log_intra, axis=-1))
        w_inter = jnp.exp(m_inter - m_t)
        p = jnp.exp(log_intra - m_t[..., None]) * jnp.einsum('bhld,bhsd->bhls', q_, k_)
        num = w_inter[..., None] * jnp.einsum('bhld,bhde->bhle', q_, c_state) + jnp.einsum('bhls,bhse->bhle', p, v_)
        den = w_inter * jnp.einsum('bhld,bhd->bhl', q_, n_state) + jnp.sum(p, axis=-1)
        h_out = num / jnp.maximum(jnp.abs(den), jnp.exp(-m_t))[..., None]
        b_end = b_[..., -1]
        log_to_end = b_end[..., None] - b_ + i_
        m_new = jnp.maximum(b_end + m_state, jnp.max(log_to_end, axis=-1))
        decay = jnp.exp(b_end + m_state - m_new)
        ka = k_ * jnp.exp(log_to_end - m_new[..., None])[..., None]
        c_new = decay[..., None, None] * c_state + jnp.einsum('bhsd,bhse->bhde', ka, v_)
        n_new = decay[..., None] * n_state + jnp.sum(ka, axis=-2)
        return (c_new, n_new, m_new), h_out

    init = (jnp.zeros((b, h, dk, dv), jnp.float32), jnp.zeros((b, h, dk), jnp.float32),
            jnp.zeros((b, h), jnp.float32))
    _, hs = lax.scan(step, init, (qc, kc, vc, bc, ic))
    return from_chunks(hs)


def retention_chunkwise(q, k, v):
    b, s, h, dk = q.shape
    size = RET_CHUNK
    log_gamma = jnp.log(1.0 - 2.0 ** (-5.0 - jnp.arange(h, dtype=jnp.float32)))
    pos = jnp.arange(size, dtype=jnp.float32)
    rel = pos[:, None] - pos[None, :]
    decay_intra = jnp.where(rel >= 0, jnp.exp(log_gamma[:, None, None] * jnp.maximum(rel, 0.0)), 0.0)
    q_decay = jnp.exp(log_gamma[:, None] * (pos + 1.0))[:, :, None]
    k_decay = jnp.exp(log_gamma[:, None] * (size - 1.0 - pos))[:, :, None]
    chunk_decay = jnp.exp(log_gamma * size)[:, None, None]
    qc = to_chunks(q, size)
    kc = to_chunks(k, size) * dk ** -0.5
    vc = to_chunks(v, size)

    def step(r_state, xs):
        q_, k_, v_ = xs
        inner = jnp.einsum('bhls,bhse->bhle', jnp.einsum('bhld,bhsd->bhls', q_, k_) * decay_intra, v_)
        cross = jnp.einsum('bhld,bhde->bhle', q_ * q_decay, r_state)
        r_new = chunk_decay * r_state + jnp.einsum('bhsd,bhse->bhde', k_ * k_decay, v_)
        return r_new, inner + cross

    init = jnp.zeros((b, h, dk, v.shape[-1]), jnp.float32)
    _, outs = lax.scan(step, init, (qc, kc, vc))
    return from_chunks(outs)


def setup_inputs(seed: int = 0) -> dict:
    key = jax.random.key(seed)
    ks = jax.random.split(key, 17)
    f32 = jnp.float32

    def normal(k, shape, scale):
        return jax.random.normal(k, shape, f32) * scale

    def gain(k, shape):
        return 1.0 + 0.05 * jax.random.normal(k, shape, f32)

    att_w = ATT_HEADS * ATT_HEAD_DIM
    mlstm_w = MLSTM_HEADS * MLSTM_V_DIM
    ret_w = RET_HEADS * RET_V_DIM
    i_bias = normal(ks[7], (DEPTH, MLSTM_HEADS), 0.1)
    f_bias = jnp.linspace(3.0, 6.0, MLSTM_HEADS, dtype=f32)[None, :] + normal(ks[8], (DEPTH, MLSTM_HEADS), 0.1)
    return {
        'x': normal(ks[0], (BATCH, SEQ, D_MODEL), 1.0),
        'norm_mix': gain(ks[1], (DEPTH, D_MODEL)),
        'norm_ffn': gain(ks[2], (DEPTH, D_MODEL)),
        'w_in': normal(ks[3], (DEPTH, D_MODEL, PROJ_DIM), D_MODEL ** -0.5),
        'att_q_norm': gain(ks[4], (DEPTH, ATT_HEAD_DIM)),
        'att_k_norm': gain(ks[5], (DEPTH, ATT_HEAD_DIM)),
        'mlstm_conv': normal(ks[6], (DEPTH, CONV_WIDTH, 2 * MLSTM_HEADS * MLSTM_QK_DIM), CONV_WIDTH ** -0.5),
        'mlstm_gate_bias': jnp.concatenate([i_bias, f_bias], axis=-1),
        'mlstm_out_norm': gain(ks[9], (DEPTH, MLSTM_HEADS, MLSTM_V_DIM)),
        'ret_out_norm': gain(ks[10], (DEPTH, RET_HEADS, RET_V_DIM)),
        'w_branch_att': normal(ks[11], (DEPTH, att_w, D_MODEL), att_w ** -0.5),
        'w_branch_mlstm': normal(ks[12], (DEPTH, mlstm_w, D_MODEL), mlstm_w ** -0.5),
        'w_branch_ret': normal(ks[13], (DEPTH, ret_w, D_MODEL), ret_w ** -0.5),
        'w_out': normal(ks[14], (DEPTH, D_MODEL, D_MODEL), D_MODEL ** -0.5),
        'w_ffn_in': normal(ks[15], (DEPTH, D_MODEL, 2 * FFN_DIM), D_MODEL ** -0.5),
        'w_ffn_out': normal(ks[16], (DEPTH, FFN_DIM, D_MODEL), FFN_DIM ** -0.5),
    }


def reference(x, norm_mix, norm_ffn, w_in, att_q_norm, att_k_norm, mlstm_conv, mlstm_gate_bias,
              mlstm_out_norm, ret_out_norm, w_branch_att, w_branch_mlstm, w_branch_ret, w_out,
              w_ffn_in, w_ffn_out):
    b, s, _ = x.shape
    pos = jnp.arange(s, dtype=jnp.int32)
    split_points = np.cumsum(SPLIT_SIZES)[:-1]
    for layer in range(DEPTH):
        h = rms_norm(x, norm_mix[layer])
        (aq, ak, av, iq, ik, iw, mq, mk, mv, mi, mf, mo,
         rq, rk, rv, rg, gates) = jnp.split(h @ w_in[layer], split_points, axis=-1)

        aq = rope(rms_norm(split_heads(aq, ATT_HEADS), att_q_norm[layer]), pos)
        ak = rope(rms_norm(split_heads(ak, ATT_HEADS), att_k_norm[layer]), pos)
        iq = rope(split_heads(iq, IDX_HEADS), pos)
        ik = rope(ik[:, :, None, :], pos)[:, :, 0, :]
        y_att = dsa_sparse_attention(aq, ak, split_heads(av, ATT_HEADS), iq, ik, iw)

        mqk = causal_depthwise_conv(jnp.concatenate([mq, mk], axis=-1), mlstm_conv[layer])
        mq, mk = jnp.split(mqk, 2, axis=-1)
        gate_bias = mlstm_gate_bias[layer]
        hm = mlstm_chunkwise(split_heads(mq, MLSTM_HEADS), split_heads(mk, MLSTM_HEADS),
                             split_heads(mv, MLSTM_HEADS),
                             mi.astype(jnp.float32) + gate_bias[:MLSTM_HEADS],
                             mf.astype(jnp.float32) + gate_bias[MLSTM_HEADS:])
        y_mlstm = (rms_norm(hm, mlstm_out_norm[layer]).reshape(b, s, -1)
                   * jax.nn.sigmoid(mo.astype(jnp.float32))).astype(x.dtype)

        hr = retention_chunkwise(rope(split_heads(rq, RET_HEADS), pos), rope(split_heads(rk, RET_HEADS), pos),
                                 split_heads(rv, RET_HEADS))
        y_ret = (head_layer_norm(hr, ret_out_norm[layer]).reshape(b, s, -1)
                 * jax.nn.silu(rg.astype(jnp.float32))).astype(x.dtype)

        g_att, g_mlstm, g_ret = jnp.split(jax.nn.sigmoid(gates), N_BRANCHES, axis=-1)
        merged = (g_att * (y_att @ w_branch_att[layer])
                  + g_mlstm * (y_mlstm @ w_branch_mlstm[layer])
                  + g_ret * (y_ret @ w_branch_ret[layer]))
        x = x + merged @ w_out[layer]

        h = rms_norm(x, norm_ffn[layer])
        ff_gate, ff_up = jnp.split(h @ w_ffn_in[layer], 2, axis=-1)
        x = x + (jax.nn.silu(ff_gate) * ff_up) @ w_ffn_out[layer]
    return x
```

```python
import functools
import math

import jax
import jax.numpy as jnp
import numpy as np
from jax import lax
from jax.experimental import pallas as pl
from jax.experimental.pallas import tpu as pltpu

F32 = jnp.float32
BF16 = jnp.bfloat16

D_MODEL = 2048
ATT_HEADS = 8
ATT_HEAD_DIM = 128
IDX_HEADS = 8
IDX_DIM = 64
INDEX_TOPK = 256
MLSTM_HEADS = 4
MLSTM_QK_DIM = 128
MLSTM_V_DIM = 256
MLSTM_CHUNK = 64
CONV_WIDTH = 4
RET_HEADS = 8
RET_QK_DIM = 128
RET_V_DIM = 128
RET_CHUNK = 128
ROPE_THETA = 10000.0
NORM_EPS = 1e-6
FFN_DIM = 5632
N_BRANCHES = 3

ATT_W = ATT_HEADS * ATT_HEAD_DIM
MLSTM_QK_W = MLSTM_HEADS * MLSTM_QK_DIM
MLSTM_V_W = MLSTM_HEADS * MLSTM_V_DIM
RET_W = RET_HEADS * RET_V_DIM

_SEGMENTS = (
    ("aq", ATT_W), ("ak", ATT_W), ("av", ATT_W), ("iq", IDX_HEADS * IDX_DIM), ("ik", IDX_DIM),
    ("iw", IDX_HEADS), ("mq", MLSTM_QK_W), ("mk", MLSTM_QK_W), ("mv", MLSTM_V_W),
    ("mi", MLSTM_HEADS), ("mf", MLSTM_HEADS), ("mo", MLSTM_V_W), ("rq", RET_W), ("rk", RET_W),
    ("rv", RET_W), ("rg", RET_W), ("gates", N_BRANCHES * D_MODEL),
)
_SEG_SIZE = dict(_SEGMENTS)
_SEG_START = {}
_off = 0
for _name, _size in _SEGMENTS:
    _SEG_START[_name] = _off
    _off += _size

_MAIN_ORDER = ("aq", "ak", "av", "mv", "mo", "rq", "rk", "rv", "rg", "gates", "iq", "mq", "mk")
_TAIL_ORDER = ("ik", "iw", "mi", "mf")
_MAIN_START = {}
_off = 0
for _name in _MAIN_ORDER:
    _MAIN_START[_name] = _off
    _off += _SEG_SIZE[_name]
MAIN_W = _off
_TAIL_START = {}
_off = 0
for _name in _TAIL_ORDER:
    _TAIL_START[_name] = _off
    _off += _SEG_SIZE[_name]
LANES = 128
TAIL_W = LANES

VMEM_LIMIT = 56 * 1024 * 1024
MASK_VALUE = -1e30


def _cparams(*semantics):
    return pltpu.CompilerParams(dimension_semantics=semantics, vmem_limit_bytes=VMEM_LIMIT)


def _rms_rows(x, g):
    ms = jnp.mean(x * x, axis=-1, keepdims=True)
    return x * lax.rsqrt(ms + NORM_EPS) * g


def _norm_matmul_kernel(x_ref, g_ref, w_ref, o_ref, h_ref):
    @pl.when(pl.program_id(1) == 0)
    def _():
        h_ref[...] = _rms_rows(x_ref[...], g_ref[...]).astype(BF16)

    o_ref[...] = jnp.dot(h_ref[...], w_ref[...].astype(BF16), preferred_element_type=F32)


def norm_matmul(x, g, w, *, tm, tn):
    m, k = x.shape
    n = w.shape[1]
    return pl.pallas_call(
        _norm_matmul_kernel,
        out_shape=jax.ShapeDtypeStruct((m, n), F32),
        grid=(m // tm, n // tn),
        in_specs=[
            pl.BlockSpec((tm, k), lambda i, j: (i, 0)),
            pl.BlockSpec((1, k), lambda i, j: (0, 0)),
            pl.BlockSpec((k, tn), lambda i, j: (0, j)),
        ],
        out_specs=pl.BlockSpec((tm, tn), lambda i, j: (i, j)),
        scratch_shapes=[pltpu.VMEM((tm, k), BF16)],
        compiler_params=_cparams("parallel", "arbitrary"),
        name="norm_matmul",
    )(x, g.reshape(1, k), w)


def _norm_swiglu_kernel(x_ref, g_ref, wg_ref, wu_ref, o_ref, h_ref):
    @pl.when(pl.program_id(1) == 0)
    def _():
        h_ref[...] = _rms_rows(x_ref[...], g_ref[...]).astype(BF16)

    h = h_ref[...]
    gate = jnp.dot(h, wg_ref[...].astype(BF16), preferred_element_type=F32)
    up = jnp.dot(h, wu_ref[...].astype(BF16), preferred_element_type=F32)
    o_ref[...] = (gate * jax.nn.sigmoid(gate) * up).astype(o_ref.dtype)


def norm_swiglu(x, g, w_in, *, tm, tn):
    m, k = x.shape
    f = w_in.shape[1] // 2
    nj = f // tn
    return pl.pallas_call(
        _norm_swiglu_kernel,
        out_shape=jax.ShapeDtypeStruct((m, f), BF16),
        grid=(m // tm, nj),
        in_specs=[
            pl.BlockSpec((tm, k), lambda i, j: (i, 0)),
            pl.BlockSpec((1, k), lambda i, j: (0, 0)),
            pl.BlockSpec((k, tn), lambda i, j: (0, j)),
            pl.BlockSpec((k, tn), lambda i, j: (0, j + nj)),
        ],
        out_specs=pl.BlockSpec((tm, tn), lambda i, j: (i, j)),
        scratch_shapes=[pltpu.VMEM((tm, k), BF16)],
        compiler_params=_cparams("parallel", "arbitrary"),
        name="norm_swiglu",
    )(x, g.reshape(1, k), w_in, w_in)


def _residual_matmul_kernel(r_ref, a_ref, w_ref, o_ref):
    o_ref[...] = r_ref[...] + jnp.dot(a_ref[...], w_ref[...].astype(BF16), preferred_element_type=F32)


def residual_matmul(res, a, w, *, tm, tn):
    m, k = a.shape
    n = w.shape[1]
    return pl.pallas_call(
        _residual_matmul_kernel,
        out_shape=jax.ShapeDtypeStruct((m, n), F32),
        grid=(m // tm, n // tn),
        in_specs=[
            pl.BlockSpec((tm, tn), lambda i, j: (i, j)),
            pl.BlockSpec((tm, k), lambda i, j: (i, 0)),
            pl.BlockSpec((k, tn), lambda i, j: (0, j)),
        ],
        out_specs=pl.BlockSpec((tm, tn), lambda i, j: (i, j)),
        compiler_params=_cparams("parallel", "arbitrary"),
        name="residual_matmul",
    )(res, a, w)


def _merge_kernel(ya_ref, ym_ref, yr_ref, wa_ref, wm_ref, wr_ref, ga_ref, gm_ref, gr_ref, o_ref):
    def branch(y_ref, w_ref, g_ref):
        proj = jnp.dot(y_ref[...], w_ref[...].astype(BF16), preferred_element_type=F32)
        return jax.nn.sigmoid(g_ref[...]) * proj

    merged = branch(ya_ref, wa_ref, ga_ref) + branch(ym_ref, wm_ref, gm_ref) + branch(yr_ref, wr_ref, gr_ref)
    o_ref[...] = merged.astype(o_ref.dtype)


def gated_merge(y_att, y_mlstm, y_ret, w_att, w_mlstm, w_ret, proj, *, tm, tn):
    m, k = y_att.shape
    n = w_att.shape[1]
    gate_blk = _MAIN_START["gates"] // tn
    per_branch = n // tn
    y_spec = pl.BlockSpec((tm, k), lambda i, j: (i, 0))
    w_spec = pl.BlockSpec((k, tn), lambda i, j: (0, j))

    def g_spec(b):
        return pl.BlockSpec((tm, tn), lambda i, j: (i, gate_blk + b * per_branch + j))

    return pl.pallas_call(
        _merge_kernel,
        out_shape=jax.ShapeDtypeStruct((m, n), BF16),
        grid=(m // tm, n // tn),
        in_specs=[y_spec, y_spec, y_spec, w_spec, w_spec, w_spec, g_spec(0), g_spec(1), g_spec(2)],
        out_specs=pl.BlockSpec((tm, tn), lambda i, j: (i, j)),
        compiler_params=_cparams("parallel", "arbitrary"),
        name="gated_merge",
    )(y_att, y_mlstm, y_ret, w_att, w_mlstm, w_ret, proj, proj, proj)


INT_MIN = -(2 ** 31)


def _sortable(x):
    bits = pltpu.bitcast(x, jnp.int32)
    bits = jnp.where(bits == INT_MIN, 0, bits)
    return bits ^ ((bits >> 31) & 0x7FFFFFFF)


def _select_kernel(ik_ref, iqt_ref, wt_ref, bias_ref, key_ref, *, tq, tk, topk):
    s_len = ik_ref.shape[0]
    i = pl.program_id(0)
    q_lo = i * tq
    n_chunks = (q_lo + tq + tk - 1) // tk
    q_pos = q_lo + lax.broadcasted_iota(jnp.int32, (tk, tq), 1)
    row_iota = lax.broadcasted_iota(jnp.int32, (tk, tq), 0)

    def score_chunk(c, carry):
        k0 = pl.multiple_of(c * tk, tk)
        ik = ik_ref[pl.ds(k0, tk), :]
        acc = jnp.zeros((tk, tq), F32)
        for h in range(IDX_HEADS):
            lg = jnp.dot(ik, iqt_ref[h * IDX_DIM:(h + 1) * IDX_DIM, :], preferred_element_type=F32)
            acc = acc + wt_ref[h:h + 1, :] * jnp.maximum(lg, 0.0)
        key = _sortable(acc)
        key = jnp.where(k0 + row_iota <= q_pos, key, INT_MIN)
        key_ref[pl.ds(k0, tk), :] = key
        return carry

    lax.fori_loop(0, n_chunks, score_chunk, 0)

    def count(pred):
        def body(c, acc):
            k0 = pl.multiple_of(c * tk, tk)
            hit = pred(key_ref[pl.ds(k0, tk), :], k0).astype(jnp.int32)
            return acc + jnp.sum(hit.reshape(tk // 8, 8, tq), axis=0)

        acc = lax.fori_loop(0, n_chunks, body, jnp.zeros((8, tq), jnp.int32))
        return jnp.sum(acc, axis=0, keepdims=True)

    thr_u = jnp.zeros((1, tq), jnp.int32)
    for bit in range(31, -1, -1):
        bit_val = INT_MIN if bit == 31 else (1 << bit)
        cand_u = thr_u | bit_val
        cand = cand_u ^ INT_MIN
        n_ge = count(lambda key, k0, cand=cand: key >= cand)
        thr_u = jnp.where(n_ge >= topk, cand_u, thr_u)
    thr = thr_u ^ INT_MIN

    n_gt = count(lambda key, k0: key > thr)
    n_ge = count(lambda key, k0: key >= thr)
    need = topk - n_gt
    excess = jnp.max(n_ge - topk)

    def tie_cut():
        cut = jnp.zeros((1, tq), jnp.int32)
        n_bits = max(1, int(math.ceil(math.log2(s_len))))
        for bit in range(n_bits - 1, -1, -1):
            cand = cut | (1 << bit)
            n_before = count(lambda key, k0, cand=cand: (key == thr) & (k0 + row_iota < cand))
            cut = jnp.where(n_before < need, cand, cut)
        return cut

    cut = lax.cond(excess > 0, tie_cut, lambda: jnp.full((1, tq), s_len, jnp.int32))

    def bias_chunk(c, carry):
        k0 = pl.multiple_of(c * tk, tk)
        key = key_ref[pl.ds(k0, tk), :]
        k_pos = k0 + row_iota
        sel = (key > thr) | ((key == thr) & (k_pos <= cut))
        sel = sel & (k_pos <= q_pos)
        bias_ref[pl.ds(k0, tk), :] = jnp.where(sel, 0.0, MASK_VALUE).astype(bias_ref.dtype)
        return carry

    lax.fori_loop(0, n_chunks, bias_chunk, 0)

    def fill_chunk(c, carry):
        k0 = pl.multiple_of(c * tk, tk)
        bias_ref[pl.ds(k0, tk), :] = jnp.full((tk, tq), MASK_VALUE, bias_ref.dtype)
        return carry

    lax.fori_loop(n_chunks, s_len // tk, fill_chunk, 0)


def select_bias(ik, iqt, wt, *, tq, tk):
    s_len = ik.shape[0]
    topk = min(INDEX_TOPK, s_len // 4)
    kern = functools.partial(_select_kernel, tq=tq, tk=tk, topk=topk)
    return pl.pallas_call(
        kern,
        out_shape=jax.ShapeDtypeStruct((s_len, s_len), BF16),
        grid=(s_len // tq,),
        in_specs=[
            pl.BlockSpec((s_len, IDX_DIM), lambda i: (0, 0)),
            pl.BlockSpec((IDX_HEADS * IDX_DIM, tq), lambda i: (0, i)),
            pl.BlockSpec((IDX_HEADS, tq), lambda i: (0, i)),
        ],
        out_specs=pl.BlockSpec((s_len, tq), lambda i: (0, i)),
        scratch_shapes=[pltpu.VMEM((s_len, tq), jnp.int32)],
        compiler_params=_cparams("parallel"),
        name="select_bias",
    )(ik, iqt, wt)


def _attention_kernel(qt_ref, k_ref, vt_ref, bias_ref, o_ref, m_ref, l_ref, acc_ref, *, tq, tk, scale):
    i = pl.program_id(0)
    kb = pl.program_id(1)
    last_kb = (i * tq + tq - 1) // tk
    dh = ATT_HEAD_DIM

    @pl.when(kb == 0)
    def _():
        m_ref[...] = jnp.full(m_ref.shape, MASK_VALUE, F32)
        l_ref[...] = jnp.zeros(l_ref.shape, F32)
        acc_ref[...] = jnp.zeros(acc_ref.shape, F32)

    @pl.when(kb <= last_kb)
    def _():
        bias = bias_ref[...].astype(F32)
        for h in range(ATT_HEADS):
            hs = slice(h * dh, (h + 1) * dh)
            s = jnp.dot(k_ref[:, hs], qt_ref[hs, :], preferred_element_type=F32) * scale + bias
            m_old = m_ref[h]
            m_new = jnp.maximum(m_old, jnp.max(s, axis=0, keepdims=True))
            alpha = jnp.exp(m_old - m_new)
            p = jnp.exp(s - m_new)
            l_ref[h] = alpha * l_ref[h] + jnp.sum(p, axis=0, keepdims=True)
            acc_ref[h] = alpha * acc_ref[h] + jnp.dot(vt_ref[hs, :], p.astype(BF16),
                                                      preferred_element_type=F32)
            m_ref[h] = m_new

    @pl.when(kb == last_kb)
    def _():
        for h in range(ATT_HEADS):
            out_t = acc_ref[h] / l_ref[h]
            o_ref[:, h * dh:(h + 1) * dh] = out_t.T.astype(o_ref.dtype)


def masked_attention(qt, k, vt, bias, *, tq, tk):
    width, s_len = qt.shape
    kern = functools.partial(_attention_kernel, tq=tq, tk=tk, scale=ATT_HEAD_DIM ** -0.5)

    def kb_eff(i, kb):
        return jnp.minimum(kb, (i * tq + tq - 1) // tk)

    return pl.pallas_call(
        kern,
        out_shape=jax.ShapeDtypeStruct((s_len, width), BF16),
        grid=(s_len // tq, s_len // tk),
        in_specs=[
            pl.BlockSpec((width, tq), lambda i, kb: (0, i)),
            pl.BlockSpec((tk, width), lambda i, kb: (kb_eff(i, kb), 0)),
            pl.BlockSpec((width, tk), lambda i, kb: (0, kb_eff(i, kb))),
            pl.BlockSpec((tk, tq), lambda i, kb: (kb_eff(i, kb), i)),
        ],
        out_specs=pl.BlockSpec((tq, width), lambda i, kb: (i, 0)),
        scratch_shapes=[
            pltpu.VMEM((ATT_HEADS, 1, tq), F32),
            pltpu.VMEM((ATT_HEADS, 1, tq), F32),
            pltpu.VMEM((ATT_HEADS, ATT_HEAD_DIM, tq), F32),
        ],
        compiler_params=_cparams("parallel", "arbitrary"),
        name="masked_attention",
    )(qt, k, vt, bias)


def _mlstm_kernel(q_ref, k_ref, v_ref, o_ref, gcol_ref, grow_ref, gain_ref, y_ref,
                  c_ref, n_ref, m_ref):
    L = q_ref.shape[0]
    dk, dv, nh = MLSTM_QK_DIM, MLSTM_V_DIM, MLSTM_HEADS

    @pl.when(pl.program_id(0) == 0)
    def _():
        c_ref[...] = jnp.zeros(c_ref.shape, F32)
        n_ref[...] = jnp.zeros(n_ref.shape, F32)
        m_ref[...] = jnp.zeros(m_ref.shape, F32)

    tril = lax.broadcasted_iota(jnp.int32, (L, L), 1) <= lax.broadcasted_iota(jnp.int32, (L, L), 0)
    gcol = gcol_ref[...]
    grow = grow_ref[0]
    for h in range(nh):
        q = q_ref[:, h * dk:(h + 1) * dk]
        k = k_ref[:, h * dk:(h + 1) * dk] * (dk ** -0.5)
        v = v_ref[:, h * dv:(h + 1) * dv]
        b_col = gcol[:, h:h + 1]
        i_col = gcol[:, nh + h:nh + h + 1]
        b_row = grow[h:h + 1, :]
        i_row = grow[nh + h:nh + h + 1, :]
        m_prev = m_ref[h]
        c_prev = c_ref[h]
        n_prev = n_ref[h]

        log_intra = jnp.where(tril, b_col - b_row + i_row, -jnp.inf)
        m_inter = b_col + m_prev
        m_t = jnp.maximum(m_inter, jnp.max(log_intra, axis=-1, keepdims=True))
        w_inter = jnp.exp(m_inter - m_t)
        qb = q.astype(BF16)
        kb = k.astype(BF16)
        vb = v.astype(BF16)
        qk = lax.dot_general(qb, kb, (((1,), (1,)), ((), ())), preferred_element_type=F32)
        p = jnp.exp(log_intra - m_t) * qk
        num = w_inter * jnp.dot(qb, c_prev.astype(BF16), preferred_element_type=F32) \
            + jnp.dot(p.astype(BF16), vb, preferred_element_type=F32)
        den = w_inter * jnp.sum(q * n_prev, axis=-1, keepdims=True) + jnp.sum(p, axis=-1, keepdims=True)
        h_out = num / jnp.maximum(jnp.abs(den), jnp.exp(-m_t))

        b_end = b_row[:, L - 1:L]
        end_row = b_end - b_row + i_row
        m_new = jnp.maximum(b_end + m_prev, jnp.max(end_row, axis=-1, keepdims=True))
        decay = jnp.exp(b_end + m_prev - m_new)
        end_col = b_end - b_col + i_col
        ka = k * jnp.exp(end_col - m_new)
        c_ref[h] = decay * c_prev + jnp.dot(ka.T.astype(BF16), vb, preferred_element_type=F32)
        n_ref[h] = decay * n_prev + jnp.sum(ka, axis=0, keepdims=True)
        m_ref[h] = m_new

        gain = gain_ref[:, h * dv:(h + 1) * dv]
        normed = h_out * lax.rsqrt(jnp.mean(h_out * h_out, axis=-1, keepdims=True) + NORM_EPS) * gain
        y_ref[:, h * dv:(h + 1) * dv] = (normed * jax.nn.sigmoid(o_ref[:, h * dv:(h + 1) * dv])).astype(y_ref.dtype)


def mlstm_mixer(mq, mk, proj, gcol, grow, gain):
    s_len = mq.shape[0]
    L = MLSTM_CHUNK
    v_blk = _MAIN_START["mv"] // MLSTM_V_W
    o_blk = _MAIN_START["mo"] // MLSTM_V_W
    return pl.pallas_call(
        _mlstm_kernel,
        out_shape=jax.ShapeDtypeStruct((s_len, MLSTM_V_W), BF16),
        grid=(s_len // L,),
        in_specs=[
            pl.BlockSpec((L, MLSTM_QK_W), lambda c: (c, 0)),
            pl.BlockSpec((L, MLSTM_QK_W), lambda c: (c, 0)),
            pl.BlockSpec((L, MLSTM_V_W), lambda c: (c, v_blk)),
            pl.BlockSpec((L, MLSTM_V_W), lambda c: (c, o_blk)),
            pl.BlockSpec((L, 2 * MLSTM_HEADS), lambda c: (c, 0)),
            pl.BlockSpec((1, 2 * MLSTM_HEADS, L), lambda c: (c, 0, 0)),
            pl.BlockSpec((1, MLSTM_V_W), lambda c: (0, 0)),
        ],
        out_specs=pl.BlockSpec((L, MLSTM_V_W), lambda c: (c, 0)),
        scratch_shapes=[
            pltpu.VMEM((MLSTM_HEADS, MLSTM_QK_DIM, MLSTM_V_DIM), F32),
            pltpu.VMEM((MLSTM_HEADS, 1, MLSTM_QK_DIM), F32),
            pltpu.VMEM((MLSTM_HEADS, 1, 1), F32),
        ],
        compiler_params=_cparams("arbitrary"),
        name="mlstm_mixer",
    )(mq, mk, proj, proj, gcol, grow, gain.reshape(1, MLSTM_V_W))


def _retention_kernel(q_ref, k_ref, v_ref, g_ref, dint_ref, qdec_ref, kdec_ref, cdec_ref, gain_ref,
                      y_ref, r_ref):
    dk, dv = RET_QK_DIM, RET_V_DIM

    @pl.when(pl.program_id(0) == 0)
    def _():
        r_ref[...] = jnp.zeros(r_ref.shape, F32)

    for h in range(RET_HEADS):
        q = q_ref[:, h * dk:(h + 1) * dk]
        k = k_ref[:, h * dk:(h + 1) * dk] * (dk ** -0.5)
        vb = v_ref[:, h * dv:(h + 1) * dv].astype(BF16)
        r_prev = r_ref[h]
        qk = lax.dot_general(q.astype(BF16), k.astype(BF16), (((1,), (1,)), ((), ())),
                             preferred_element_type=F32)
        inner = jnp.dot((qk * dint_ref[h]).astype(BF16), vb, preferred_element_type=F32)
        cross = jnp.dot((q * qdec_ref[h]).astype(BF16), r_prev.astype(BF16), preferred_element_type=F32)
        out = inner + cross
        kd = k * kdec_ref[h]
        r_ref[h] = cdec_ref[h] * r_prev + jnp.dot(kd.T.astype(BF16), vb, preferred_element_type=F32)

        mu = jnp.mean(out, axis=-1, keepdims=True)
        xc = out - mu
        normed = xc * lax.rsqrt(jnp.mean(xc * xc, axis=-1, keepdims=True) + NORM_EPS)
        gate = g_ref[:, h * dv:(h + 1) * dv]
        y = normed * gain_ref[:, h * dv:(h + 1) * dv] * (gate * jax.nn.sigmoid(gate))
        y_ref[:, h * dv:(h + 1) * dv] = y.astype(y_ref.dtype)


def _retention_tables(size):
    log_gamma = jnp.log(1.0 - 2.0 ** (-5.0 - jnp.arange(RET_HEADS, dtype=F32)))
    pos = jnp.arange(size, dtype=F32)
    rel = pos[:, None] - pos[None, :]
    decay_intra = jnp.where(rel >= 0, jnp.exp(log_gamma[:, None, None] * jnp.maximum(rel, 0.0)), 0.0)
    q_decay = jnp.exp(log_gamma[:, None] * (pos + 1.0))[:, :, None]
    k_decay = jnp.exp(log_gamma[:, None] * (size - 1.0 - pos))[:, :, None]
    chunk_decay = jnp.exp(log_gamma * size)[:, None, None]
    return decay_intra, q_decay, k_decay, chunk_decay


def retention_mixer(rq, rk, proj, gain):
    s_len = rq.shape[0]
    L = RET_CHUNK
    v_blk = _MAIN_START["rv"] // RET_W
    g_blk = _MAIN_START["rg"] // RET_W
    decay_intra, q_decay, k_decay, chunk_decay = _retention_tables(L)
    nh = RET_HEADS
    return pl.pallas_call(
        _retention_kernel,
        out_shape=jax.ShapeDtypeStruct((s_len, RET_W), BF16),
        grid=(s_len // L,),
        in_specs=[
            pl.BlockSpec((L, RET_W), lambda c: (c, 0)),
            pl.BlockSpec((L, RET_W), lambda c: (c, 0)),
            pl.BlockSpec((L, RET_W), lambda c: (c, v_blk)),
            pl.BlockSpec((L, RET_W), lambda c: (c, g_blk)),
            pl.BlockSpec((nh, L, L), lambda c: (0, 0, 0)),
            pl.BlockSpec((nh, L, 1), lambda c: (0, 0, 0)),
            pl.BlockSpec((nh, L, 1), lambda c: (0, 0, 0)),
            pl.BlockSpec((nh, 1, 1), lambda c: (0, 0, 0)),
            pl.BlockSpec((1, RET_W), lambda c: (0, 0)),
        ],
        out_specs=pl.BlockSpec((L, RET_W), lambda c: (c, 0)),
        scratch_shapes=[pltpu.VMEM((nh, RET_QK_DIM, RET_V_DIM), F32)],
        compiler_params=_cparams("arbitrary"),
        name="retention_mixer",
    )(rq, rk, proj, proj, decay_intra, q_decay, k_decay, chunk_decay, gain.reshape(1, RET_W))


def _rope_tables(s_len, d):
    half = d // 2
    inv_freq = ROPE_THETA ** (-jnp.arange(half, dtype=F32) * 2.0 / d)
    ang = jnp.arange(s_len, dtype=jnp.int32).astype(F32)[:, None] * inv_freq[None, :]
    return jnp.cos(ang)[:, None, :], jnp.sin(ang)[:, None, :]


def _rope(x, cos, sin):
    half = x.shape[-1] // 2
    x1, x2 = x[..., :half], x[..., half:]
    return jnp.concatenate([x1 * cos - x2 * sin, x2 * cos + x1 * sin], axis=-1)


def _head_rms(x, g):
    return x * lax.rsqrt(jnp.mean(x * x, axis=-1, keepdims=True) + NORM_EPS) * g


def _causal_conv(x, w):
    width = w.shape[0]
    out = x * w[width - 1]
    for j in range(width - 1):
        shift = width - 1 - j
        out = out + jnp.pad(x, ((shift, 0), (0, 0)))[:x.shape[0]] * w[j]
    return out


def _permuted_weights(w_in_layer):
    cols = lambda name: w_in_layer[:, _SEG_START[name]:_SEG_START[name] + _SEG_SIZE[name]]
    w_main = jnp.concatenate([cols(n) for n in _MAIN_ORDER], axis=1)
    tail_used = sum(_SEG_SIZE[n] for n in _TAIL_ORDER)
    w_tail = jnp.concatenate([cols(n) for n in _TAIL_ORDER]
                             + [jnp.zeros((w_in_layer.shape[0], TAIL_W - tail_used), w_in_layer.dtype)], axis=1)
    return w_main, w_tail


def _tile(n, pref):
    t = min(n, pref)
    assert n % t == 0, (n, pref)
    return t


def _layer(x, norm_mix, norm_ffn, w_in, att_q_norm, att_k_norm, mlstm_conv, mlstm_gate_bias,
           mlstm_out_norm, ret_out_norm, w_branch_att, w_branch_mlstm, w_branch_ret, w_out,
           w_ffn_in, w_ffn_out):
    s_len = x.shape[0]
    tm = _tile(s_len, 1024)
    w_main, w_tail = _permuted_weights(w_in)
    proj = norm_matmul(x, norm_mix, w_main, tm=tm, tn=512)
    tail = norm_matmul(x, norm_mix, w_tail, tm=tm, tn=TAIL_W)

    def seg(name):
        return proj[:, _MAIN_START[name]:_MAIN_START[name] + _SEG_SIZE[name]]

    def tseg(name):
        return tail[:, _TAIL_START[name]:_TAIL_START[name] + _SEG_SIZE[name]]

    cos_a, sin_a = _rope_tables(s_len, ATT_HEAD_DIM)
    cos_i, sin_i = _rope_tables(s_len, IDX_DIM)
    aq = _rope(_head_rms(seg("aq").reshape(s_len, ATT_HEADS, ATT_HEAD_DIM), att_q_norm), cos_a, sin_a)
    ak = _rope(_head_rms(seg("ak").reshape(s_len, ATT_HEADS, ATT_HEAD_DIM), att_k_norm), cos_a, sin_a)
    qt = aq.reshape(s_len, ATT_W).T.astype(BF16)
    kk = ak.reshape(s_len, ATT_W).astype(BF16)
    vt = seg("av").T.astype(BF16)
    iq = _rope(seg("iq").reshape(s_len, IDX_HEADS, IDX_DIM), cos_i, sin_i)
    iqt = iq.reshape(s_len, IDX_HEADS * IDX_DIM).T.astype(BF16)
    ik = _rope(tseg("ik")[:, None, :], cos_i, sin_i)[:, 0, :].astype(BF16)
    wt = (tseg("iw") * (IDX_HEADS ** -0.5 * IDX_DIM ** -0.5)).T
    tq = _tile(s_len, 256)
    bias = select_bias(ik, iqt, wt, tq=tq, tk=_tile(s_len, 128))
    y_att = masked_attention(qt, kk, vt, bias, tq=tq, tk=_tile(s_len, 256))

    mqk = _causal_conv(jnp.concatenate([seg("mq"), seg("mk")], axis=-1), mlstm_conv)
    mq, mk = mqk[:, :MLSTM_QK_W], mqk[:, MLSTM_QK_W:]
    i_pre = tseg("mi") + mlstm_gate_bias[:MLSTM_HEADS]
    f_pre = tseg("mf") + mlstm_gate_bias[MLSTM_HEADS:]
    n_chunks = s_len // MLSTM_CHUNK
    logf = jax.nn.log_sigmoid(f_pre).reshape(n_chunks, MLSTM_CHUNK, MLSTM_HEADS)
    b_cum = jnp.cumsum(logf, axis=1).reshape(s_len, MLSTM_HEADS)
    gcol = jnp.concatenate([b_cum, i_pre], axis=-1)
    grow = gcol.reshape(n_chunks, MLSTM_CHUNK, 2 * MLSTM_HEADS).transpose(0, 2, 1)
    y_mlstm = mlstm_mixer(mq, mk, proj, gcol, grow, mlstm_out_norm.reshape(-1))

    rq = _rope(seg("rq").reshape(s_len, RET_HEADS, RET_QK_DIM), cos_a, sin_a).reshape(s_len, RET_W)
    rk = _rope(seg("rk").reshape(s_len, RET_HEADS, RET_QK_DIM), cos_a, sin_a).reshape(s_len, RET_W)
    y_ret = retention_mixer(rq, rk, proj, ret_out_norm.reshape(-1))

    merged = gated_merge(y_att, y_mlstm, y_ret, w_branch_att, w_branch_mlstm, w_branch_ret, proj,
                         tm=tm, tn=512)
    x = residual_matmul(x, merged, w_out, tm=tm, tn=512)

    act = norm_swiglu(x, norm_ffn, w_ffn_in, tm=tm, tn=512)
    x = residual_matmul(x, act, w_ffn_out, tm=tm, tn=256)
    return x


def kernel(x, norm_mix, norm_ffn, w_in, att_q_norm, att_k_norm, mlstm_conv, mlstm_gate_bias,
           mlstm_out_norm, ret_out_norm, w_branch_att, w_branch_mlstm, w_branch_ret, w_out,
           w_ffn_in, w_ffn_out):
    b, s_len, d = x.shape
    assert d == D_MODEL and s_len % 256 == 0
    outs = []
    for bi in range(b):
        xb = x[bi]
        for layer in range(w_in.shape[0]):
            xb = _layer(xb, norm_mix[layer], norm_ffn[layer], w_in[layer], att_q_norm[layer],
                        att_k_norm[layer], mlstm_conv[layer], mlstm_gate_bias[layer],
                        mlstm_out_norm[layer], ret_out_norm[layer], w_branch_att[layer],
                        w_branch_mlstm[layer], w_branch_ret[layer], w_out[layer], w_ffn_in[layer],
                        w_ffn_out[layer])
        outs.append(xb)
    return jnp.stack(outs, axis=0)
```

```python
import functools
import math

import jax
import jax.numpy as jnp
import numpy as np
from jax import lax
from jax.experimental import pallas as pl
from jax.experimental.pallas import tpu as pltpu

F32 = jnp.float32
BF16 = jnp.bfloat16

D_MODEL = 2048
ATT_HEADS = 8
ATT_HEAD_DIM = 128
IDX_HEADS = 8
IDX_DIM = 64
INDEX_TOPK = 256
MLSTM_HEADS = 4
MLSTM_QK_DIM = 128
MLSTM_V_DIM = 256
MLSTM_CHUNK = 64
CONV_WIDTH = 4
RET_HEADS = 8
RET_QK_DIM = 128
RET_V_DIM = 128
RET_CHUNK = 128
ROPE_THETA = 10000.0
NORM_EPS = 1e-6
FFN_DIM = 5632
N_BRANCHES = 3

ATT_W = ATT_HEADS * ATT_HEAD_DIM
MLSTM_QK_W = MLSTM_HEADS * MLSTM_QK_DIM
MLSTM_V_W = MLSTM_HEADS * MLSTM_V_DIM
RET_W = RET_HEADS * RET_V_DIM

_SEGMENTS = (
    ("aq", ATT_W), ("ak", ATT_W), ("av", ATT_W), ("iq", IDX_HEADS * IDX_DIM), ("ik", IDX_DIM),
    ("iw", IDX_HEADS), ("mq", MLSTM_QK_W), ("mk", MLSTM_QK_W), ("mv", MLSTM_V_W),
    ("mi", MLSTM_HEADS), ("mf", MLSTM_HEADS), ("mo", MLSTM_V_W), ("rq", RET_W), ("rk", RET_W),
    ("rv", RET_W), ("rg", RET_W), ("gates", N_BRANCHES * D_MODEL),
)
_SEG_SIZE = dict(_SEGMENTS)
_SEG_START = {}
_off = 0
for _name, _size in _SEGMENTS:
    _SEG_START[_name] = _off
    _off += _size

_MAIN_ORDER = ("aq", "ak", "av", "mv", "mo", "rq", "rk", "rv", "rg", "gates", "iq", "mq", "mk")
_TAIL_ORDER = ("ik", "iw", "mi", "mf")
_MAIN_START = {}
_off = 0
for _name in _MAIN_ORDER:
    _MAIN_START[_name] = _off
    _off += _SEG_SIZE[_name]
MAIN_W = _off
_TAIL_START = {}
_off = 0
for _name in _TAIL_ORDER:
    _TAIL_START[_name] = _off
    _off += _SEG_SIZE[_name]
LANES = 128
TAIL_W = LANES

VMEM_LIMIT = 56 * 1024 * 1024
MASK_VALUE = -1e30


def _cparams(*semantics):
    return pltpu.CompilerParams(dimension_semantics=semantics, vmem_limit_bytes=VMEM_LIMIT)


def _rms_rows(x, g):
    ms = jnp.mean(x * x, axis=-1, keepdims=True)
    return x * lax.rsqrt(ms + NORM_EPS) * g


def _norm_matmul_kernel(x_ref, g_ref, w_ref, o_ref, h_ref):
    @pl.when(pl.program_id(1) == 0)
    def _():
        h_ref[...] = _rms_rows(x_ref[...], g_ref[...]).astype(BF16)

    o_ref[...] = jnp.dot(h_ref[...], w_ref[...].astype(BF16), preferred_element_type=F32)


def norm_matmul(x, g, w, *, tm, tn):
    m, k = x.shape
    n = w.shape[1]
    return pl.pallas_call(
        _norm_matmul_kernel,
        out_shape=jax.ShapeDtypeStruct((m, n), F32),
        grid=(m // tm, n // tn),
        in_specs=[
            pl.BlockSpec((tm, k), lambda i, j: (i, 0)),
            pl.BlockSpec((1, k), lambda i, j: (0, 0)),
            pl.BlockSpec((k, tn), lambda i, j: (0, j)),
        ],
        out_specs=pl.BlockSpec((tm, tn), lambda i, j: (i, j)),
        scratch_shapes=[pltpu.VMEM((tm, k), BF16)],
        compiler_params=_cparams("parallel", "arbitrary"),
        name="norm_matmul",
    )(x, g.reshape(1, k), w)


def _norm_swiglu_kernel(x_ref, g_ref, wg_ref, wu_ref, o_ref, h_ref):
    @pl.when(pl.program_id(1) == 0)
    def _():
        h_ref[...] = _rms_rows(x_ref[...], g_ref[...]).astype(BF16)

    h = h_ref[...]
    gate = jnp.dot(h, wg_ref[...].astype(BF16), preferred_element_type=F32)
    up = jnp.dot(h, wu_ref[...].astype(BF16), preferred_element_type=F32)
    o_ref[...] = (gate * jax.nn.sigmoid(gate) * up).astype(o_ref.dtype)


def norm_swiglu(x, g, w_in, *, tm, tn):
    m, k = x.shape
    f = w_in.shape[1] // 2
    nj = f // tn
    return pl.pallas_call(
        _norm_swiglu_kernel,
        out_shape=jax.ShapeDtypeStruct((m, f), BF16),
        grid=(m // tm, nj),
        in_specs=[
            pl.BlockSpec((tm, k), lambda i, j: (i, 0)),
            pl.BlockSpec((1, k), lambda i, j: (0, 0)),
            pl.BlockSpec((k, tn), lambda i, j: (0, j)),
            pl.BlockSpec((k, tn), lambda i, j: (0, j + nj)),
        ],
        out_specs=pl.BlockSpec((tm, tn), lambda i, j: (i, j)),
        scratch_shapes=[pltpu.VMEM((tm, k), BF16)],
        compiler_params=_cparams("parallel", "arbitrary"),
        name="norm_swiglu",
    )(x, g.reshape(1, k), w_in, w_in)


def _residual_matmul_kernel(r_ref, a_ref, w_ref, o_ref):
    o_ref[...] = r_ref[...] + jnp.dot(a_ref[...], w_ref[...].astype(BF16), preferred_element_type=F32)


def residual_matmul(res, a, w, *, tm, tn):
    m, k = a.shape
    n = w.shape[1]
    return pl.pallas_call(
        _residual_matmul_kernel,
        out_shape=jax.ShapeDtypeStruct((m, n), F32),
        grid=(m // tm, n // tn),
        in_specs=[
            pl.BlockSpec((tm, tn), lambda i, j: (i, j)),
            pl.BlockSpec((tm, k), lambda i, j: (i, 0)),
            pl.BlockSpec((k, tn), lambda i, j: (0, j)),
        ],
        out_specs=pl.BlockSpec((tm, tn), lambda i, j: (i, j)),
        compiler_params=_cparams("parallel", "arbitrary"),
        name="residual_matmul",
    )(res, a, w)


def _merge_kernel(ya_ref, ym_ref, yr_ref, wa_ref, wm_ref, wr_ref, ga_ref, gm_ref, gr_ref, o_ref):
    def branch(y_ref, w_ref, g_ref):
        proj = jnp.dot(y_ref[...], w_ref[...].astype(BF16), preferred_element_type=F32)
        return jax.nn.sigmoid(g_ref[...]) * proj

    merged = branch(ya_ref, wa_ref, ga_ref) + branch(ym_ref, wm_ref, gm_ref) + branch(yr_ref, wr_ref, gr_ref)
    o_ref[...] = merged.astype(o_ref.dtype)


def gated_merge(y_att, y_mlstm, y_ret, w_att, w_mlstm, w_ret, proj, *, tm, tn):
    m, k = y_att.shape
    n = w_att.shape[1]
    gate_blk = _MAIN_START["gates"] // tn
    per_branch = n // tn
    y_spec = pl.BlockSpec((tm, k), lambda i, j: (i, 0))
    w_spec = pl.BlockSpec((k, tn), lambda i, j: (0, j))

    def g_spec(b):
        return pl.BlockSpec((tm, tn), lambda i, j: (i, gate_blk + b * per_branch + j))

    return pl.pallas_call(
        _merge_kernel,
        out_shape=jax.ShapeDtypeStruct((m, n), BF16),
        grid=(m // tm, n // tn),
        in_specs=[y_spec, y_spec, y_spec, w_spec, w_spec, w_spec, g_spec(0), g_spec(1), g_spec(2)],
        out_specs=pl.BlockSpec((tm, tn), lambda i, j: (i, j)),
        compiler_params=_cparams("parallel", "arbitrary"),
        name="gated_merge",
    )(y_att, y_mlstm, y_ret, w_att, w_mlstm, w_ret, proj, proj, proj)


INT_MIN = -(2 ** 31)


def _sortable(x):
    bits = pltpu.bitcast(x, jnp.int32)
    bits = jnp.where(bits == INT_MIN, 0, bits)
    return bits ^ ((bits >> 31) & 0x7FFFFFFF)


GROUP = 256


def _bit_transpose32(words):
    a = list(words)
    j, m = 16, 0x0000FFFF
    while j:
        k = 0
        while k < 32:
            t = (a[k] ^ lax.shift_right_logical(a[k + j], jnp.int32(j))) & m
            a[k] = a[k] ^ t
            a[k + j] = a[k + j] ^ (t << j)
            k = (k + j + 1) & ~j
        j >>= 1
        m = (m ^ (m << j)) & 0xFFFFFFFF
    return a


def _select_kernel(ik_ref, iqt_ref, wt_ref, bias_ref, key_ref, plane_ref, alive_ref, *, tq, tk, topk):
    s_len = ik_ref.shape[0]
    i = pl.program_id(0)
    q_lo = i * tq
    n_groups = (q_lo + tq + GROUP - 1) // GROUP
    n_chunks = n_groups * (GROUP // tk)
    q_pos = q_lo + lax.broadcasted_iota(jnp.int32, (tk, tq), 1)
    row_iota = lax.broadcasted_iota(jnp.int32, (tk, tq), 0)

    @pl.when(i == 0)
    def _():
        plane_ref[...] = jnp.zeros(plane_ref.shape, jnp.int32)

    def score_group(g, carry):
        g0 = pl.multiple_of(g * GROUP, GROUP)
        slabs = []
        for sub in range(GROUP // tk):
            k0 = g0 + sub * tk
            ik = ik_ref[pl.ds(k0, tk), :]
            acc = jnp.zeros((tk, tq), F32)
            for h in range(IDX_HEADS):
                lg = jnp.dot(ik, iqt_ref[h * IDX_DIM:(h + 1) * IDX_DIM, :], preferred_element_type=F32)
                acc = acc + wt_ref[h:h + 1, :] * jnp.maximum(lg, 0.0)
            key = _sortable(acc)
            key = jnp.where(k0 + row_iota <= q_pos, key, INT_MIN)
            key_ref[pl.ds(k0, tk), :] = key
            slabs += [key[8 * j:8 * j + 8, :] for j in range(tk // 8)]
        planes = _bit_transpose32(slabs)
        r0 = pl.multiple_of(g * 8, 8)
        for p in range(32):
            word = planes[31 - p]
            plane_ref[p, pl.ds(r0, 8), :] = ~word if p == 31 else word
        return carry

    lax.fori_loop(0, n_groups, score_group, 0)

    grp = lax.broadcasted_iota(jnp.int32, alive_ref.shape, 0) >> 3
    alive_ref[...] = jnp.where(grp < n_groups, -1, 0)
    n_gt = jnp.zeros((1, tq), jnp.int32)
    thr_u = jnp.zeros((1, tq), jnp.int32)
    for p in range(31, -1, -1):
        alive = alive_ref[...]
        ones = alive & plane_ref[p]
        c = jnp.sum(lax.population_count(ones), axis=0, keepdims=True)
        take1 = (n_gt + c) >= topk
        alive_ref[...] = jnp.where(take1, ones, alive ^ ones)
        n_gt = n_gt + jnp.where(take1, 0, c)
        thr_u = thr_u | jnp.where(take1, INT_MIN if p == 31 else (1 << p), 0)
    thr = thr_u ^ INT_MIN
    n_eq = jnp.sum(lax.population_count(alive_ref[...]), axis=0, keepdims=True)

    need = topk - n_gt
    excess = jnp.max(n_eq - need)

    def count(pred):
        def body(c, acc):
            k0 = pl.multiple_of(c * tk, tk)
            hit = pred(key_ref[pl.ds(k0, tk), :], k0).astype(jnp.int32)
            return acc + jnp.sum(hit.reshape(tk // 8, 8, tq), axis=0)

        acc = lax.fori_loop(0, n_chunks, body, jnp.zeros((8, tq), jnp.int32))
        return jnp.sum(acc, axis=0, keepdims=True)

    def tie_cut():
        cut = jnp.zeros((1, tq), jnp.int32)
        n_bits = max(1, int(math.ceil(math.log2(s_len))))
        for bit in range(n_bits - 1, -1, -1):
            cand = cut | (1 << bit)
            n_before = count(lambda key, k0, cand=cand: (key == thr) & (k0 + row_iota < cand))
            cut = jnp.where(n_before < need, cand, cut)
        return cut

    cut = lax.cond(excess > 0, tie_cut, lambda: jnp.full((1, tq), s_len, jnp.int32))

    def bias_chunk(c, carry):
        k0 = pl.multiple_of(c * tk, tk)
        key = key_ref[pl.ds(k0, tk), :]
        k_pos = k0 + row_iota
        sel = (key > thr) | ((key == thr) & (k_pos <= cut))
        sel = sel & (k_pos <= q_pos)
        bias_ref[pl.ds(k0, tk), :] = jnp.where(sel, 0.0, MASK_VALUE).astype(bias_ref.dtype)
        return carry

    lax.fori_loop(0, n_chunks, bias_chunk, 0)

    def fill_chunk(c, carry):
        k0 = pl.multiple_of(c * tk, tk)
        bias_ref[pl.ds(k0, tk), :] = jnp.full((tk, tq), MASK_VALUE, bias_ref.dtype)
        return carry

    lax.fori_loop(n_chunks, s_len // tk, fill_chunk, 0)


def select_bias(ik, iqt, wt, *, tq, tk):
    s_len = ik.shape[0]
    topk = min(INDEX_TOPK, s_len // 4)
    kern = functools.partial(_select_kernel, tq=tq, tk=tk, topk=topk)
    return pl.pallas_call(
        kern,
        out_shape=jax.ShapeDtypeStruct((s_len, s_len), BF16),
        grid=(s_len // tq,),
        in_specs=[
            pl.BlockSpec((s_len, IDX_DIM), lambda i: (0, 0)),
            pl.BlockSpec((IDX_HEADS * IDX_DIM, tq), lambda i: (0, i)),
            pl.BlockSpec((IDX_HEADS, tq), lambda i: (0, i)),
        ],
        out_specs=pl.BlockSpec((s_len, tq), lambda i: (0, i)),
        scratch_shapes=[
            pltpu.VMEM((s_len, tq), jnp.int32),
            pltpu.VMEM((32, s_len // 32, tq), jnp.int32),
            pltpu.VMEM((s_len // 32, tq), jnp.int32),
        ],
        compiler_params=_cparams("arbitrary"),
        name="select_bias",
    )(ik, iqt, wt)


ATT_Q_SCALE = ATT_HEAD_DIM ** -0.5 * math.log2(math.e)


def _attention_kernel(qi_ref, ki_ref, qt_ref, k_ref, vt_ref, bias_ref, o_ref, m_ref, l_ref, acc_ref,
                      s_ref, *, tq, tk):
    step = pl.program_id(0)
    i = qi_ref[step]
    kb = ki_ref[step]
    last_kb = (i * tq + tq - 1) // tk
    dh = ATT_HEAD_DIM

    @pl.when(kb == 0)
    def _():
        m_ref[...] = jnp.full(m_ref.shape, MASK_VALUE, F32)
        l_ref[...] = jnp.zeros(l_ref.shape, F32)
        acc_ref[...] = jnp.zeros(acc_ref.shape, F32)

    bias = bias_ref[...].astype(F32)

    m_new = []
    for h in range(ATT_HEADS):
        hs = slice(h * dh, (h + 1) * dh)
        s = jnp.dot(k_ref[:, hs], qt_ref[hs, :], preferred_element_type=F32) + bias
        s_ref[h] = s
        m_new.append(jnp.maximum(m_ref[h], jnp.max(s, axis=0, keepdims=True)))
    for h in range(ATT_HEADS):
        hs = slice(h * dh, (h + 1) * dh)
        alpha = jnp.exp2(m_ref[h] - m_new[h])
        p = jnp.exp2(s_ref[h] - m_new[h])
        l_ref[h] = alpha * l_ref[h] + jnp.sum(p, axis=0, keepdims=True)
        acc_ref[h] = alpha * acc_ref[h] + jnp.dot(vt_ref[hs, :], p.astype(BF16),
                                                  preferred_element_type=F32)
        m_ref[h] = m_new[h]

    @pl.when(kb == last_kb)
    def _():
        for h in range(ATT_HEADS):
            out_t = acc_ref[h] / l_ref[h]
            o_ref[:, h * dh:(h + 1) * dh] = out_t.T.astype(o_ref.dtype)


def masked_attention(qt, k, vt, bias, *, tq, tk):
    width, s_len = qt.shape
    pairs = [(i, kb) for i in range(s_len // tq) for kb in range((i * tq + tq - 1) // tk + 1)]
    qi = jnp.asarray([p[0] for p in pairs], jnp.int32)
    ki = jnp.asarray([p[1] for p in pairs], jnp.int32)
    grid_spec = pltpu.PrefetchScalarGridSpec(
        num_scalar_prefetch=2,
        grid=(len(pairs),),
        in_specs=[
            pl.BlockSpec((width, tq), lambda s, qi, ki: (0, qi[s])),
            pl.BlockSpec((tk, width), lambda s, qi, ki: (ki[s], 0)),
            pl.BlockSpec((width, tk), lambda s, qi, ki: (0, ki[s])),
            pl.BlockSpec((tk, tq), lambda s, qi, ki: (ki[s], qi[s])),
        ],
        out_specs=pl.BlockSpec((tq, width), lambda s, qi, ki: (qi[s], 0)),
        scratch_shapes=[
            pltpu.VMEM((ATT_HEADS, 1, tq), F32),
            pltpu.VMEM((ATT_HEADS, 1, tq), F32),
            pltpu.VMEM((ATT_HEADS, ATT_HEAD_DIM, tq), F32),
            pltpu.VMEM((ATT_HEADS, tk, tq), F32),
        ],
    )
    return pl.pallas_call(
        functools.partial(_attention_kernel, tq=tq, tk=tk),
        out_shape=jax.ShapeDtypeStruct((s_len, width), BF16),
        grid_spec=grid_spec,
        compiler_params=_cparams("arbitrary"),
        name="masked_attention",
    )(qi, ki, qt, k, vt, bias)


def _mlstm_kernel(q_ref, k_ref, v_ref, o_ref, gcol_ref, grow_ref, gain_ref, y_ref,
                  c_ref, n_ref, m_ref):
    L = q_ref.shape[0]
    dk, dv, nh = MLSTM_QK_DIM, MLSTM_V_DIM, MLSTM_HEADS

    @pl.when(pl.program_id(0) == 0)
    def _():
        c_ref[...] = jnp.zeros(c_ref.shape, F32)
        n_ref[...] = jnp.zeros(n_ref.shape, F32)
        m_ref[...] = jnp.zeros(m_ref.shape, F32)

    tril = lax.broadcasted_iota(jnp.int32, (L, L), 1) <= lax.broadcasted_iota(jnp.int32, (L, L), 0)
    gcol = gcol_ref[...]
    grow = grow_ref[0]
    for h in range(nh):
        q = q_ref[:, h * dk:(h + 1) * dk]
        k = k_ref[:, h * dk:(h + 1) * dk] * (dk ** -0.5)
        v = v_ref[:, h * dv:(h + 1) * dv]
        b_col = gcol[:, h:h + 1]
        i_col = gcol[:, nh + h:nh + h + 1]
        b_row = grow[h:h + 1, :]
        i_row = grow[nh + h:nh + h + 1, :]
        m_prev = m_ref[h]
        c_prev = c_ref[h]
        n_prev = n_ref[h]

        log_intra = jnp.where(tril, b_col - b_row + i_row, -jnp.inf)
        m_inter = b_col + m_prev
        m_t = jnp.maximum(m_inter, jnp.max(log_intra, axis=-1, keepdims=True))
        w_inter = jnp.exp(m_inter - m_t)
        qb = q.astype(BF16)
        kb = k.astype(BF16)
        vb = v.astype(BF16)
        qk = lax.dot_general(qb, kb, (((1,), (1,)), ((), ())), preferred_element_type=F32)
        p = jnp.exp(log_intra - m_t) * qk
        num = w_inter * jnp.dot(qb, c_prev.astype(BF16), preferred_element_type=F32) \
            + jnp.dot(p.astype(BF16), vb, preferred_element_type=F32)
        den = w_inter * jnp.sum(q * n_prev, axis=-1, keepdims=True) + jnp.sum(p, axis=-1, keepdims=True)
        h_out = num / jnp.maximum(jnp.abs(den), jnp.exp(-m_t))

        b_end = b_row[:, L - 1:L]
        end_row = b_end - b_row + i_row
        m_new = jnp.maximum(b_end + m_prev, jnp.max(end_row, axis=-1, keepdims=True))
        decay = jnp.exp(b_end + m_prev - m_new)
        end_col = b_end - b_col + i_col
        ka = k * jnp.exp(end_col - m_new)
        c_ref[h] = decay * c_prev + jnp.dot(ka.T.astype(BF16), vb, preferred_element_type=F32)
        n_ref[h] = decay * n_prev + jnp.sum(ka, axis=0, keepdims=True)
        m_ref[h] = m_new

        gain = gain_ref[:, h * dv:(h + 1) * dv]
        normed = h_out * lax.rsqrt(jnp.mean(h_out * h_out, axis=-1, keepdims=True) + NORM_EPS) * gain
        y_ref[:, h * dv:(h + 1) * dv] = (normed * jax.nn.sigmoid(o_ref[:, h * dv:(h + 1) * dv])).astype(y_ref.dtype)


def mlstm_mixer(mq, mk, proj, gcol, grow, gain):
    s_len = mq.shape[0]
    L = MLSTM_CHUNK
    v_blk = _MAIN_START["mv"] // MLSTM_V_W
    o_blk = _MAIN_START["mo"] // MLSTM_V_W
    return pl.pallas_call(
        _mlstm_kernel,
        out_shape=jax.ShapeDtypeStruct((s_len, MLSTM_V_W), BF16),
        grid=(s_len // L,),
        in_specs=[
            pl.BlockSpec((L, MLSTM_QK_W), lambda c: (c, 0)),
            pl.BlockSpec((L, MLSTM_QK_W), lambda c: (c, 0)),
            pl.BlockSpec((L, MLSTM_V_W), lambda c: (c, v_blk)),
            pl.BlockSpec((L, MLSTM_V_W), lambda c: (c, o_blk)),
            pl.BlockSpec((L, 2 * MLSTM_HEADS), lambda c: (c, 0)),
            pl.BlockSpec((1, 2 * MLSTM_HEADS, L), lambda c: (c, 0, 0)),
            pl.BlockSpec((1, MLSTM_V_W), lambda c: (0, 0)),
        ],
        out_specs=pl.BlockSpec((L, MLSTM_V_W), lambda c: (c, 0)),
        scratch_shapes=[
            pltpu.VMEM((MLSTM_HEADS, MLSTM_QK_DIM, MLSTM_V_DIM), F32),
            pltpu.VMEM((MLSTM_HEADS, 1, MLSTM_QK_DIM), F32),
            pltpu.VMEM((MLSTM_HEADS, 1, 1), F32),
        ],
        compiler_params=_cparams("arbitrary"),
        name="mlstm_mixer",
    )(mq, mk, proj, proj, gcol, grow, gain.reshape(1, MLSTM_V_W))


def _retention_kernel(q_ref, k_ref, v_ref, g_ref, dint_ref, qdec_ref, kdec_ref, cdec_ref, gain_ref,
                      y_ref, r_ref):
    dk, dv = RET_QK_DIM, RET_V_DIM

    @pl.when(pl.program_id(0) == 0)
    def _():
        r_ref[...] = jnp.zeros(r_ref.shape, F32)

    for h in range(RET_HEADS):
        q = q_ref[:, h * dk:(h + 1) * dk]
        k = k_ref[:, h * dk:(h + 1) * dk] * (dk ** -0.5)
        vb = v_ref[:, h * dv:(h + 1) * dv].astype(BF16)
        r_prev = r_ref[h]
        qk = lax.dot_general(q.astype(BF16), k.astype(BF16), (((1,), (1,)), ((), ())),
                             preferred_element_type=F32)
        inner = jnp.dot((qk * dint_ref[h]).astype(BF16), vb, preferred_element_type=F32)
        cross = jnp.dot((q * qdec_ref[h]).astype(BF16), r_prev.astype(BF16), preferred_element_type=F32)
        out = inner + cross
        kd = k * kdec_ref[h]
        r_ref[h] = cdec_ref[h] * r_prev + jnp.dot(kd.T.astype(BF16), vb, preferred_element_type=F32)

        mu = jnp.mean(out, axis=-1, keepdims=True)
        xc = out - mu
        normed = xc * lax.rsqrt(jnp.mean(xc * xc, axis=-1, keepdims=True) + NORM_EPS)
        gate = g_ref[:, h * dv:(h + 1) * dv]
        y = normed * gain_ref[:, h * dv:(h + 1) * dv] * (gate * jax.nn.sigmoid(gate))
        y_ref[:, h * dv:(h + 1) * dv] = y.astype(y_ref.dtype)


def _retention_tables(size):
    log_gamma = jnp.log(1.0 - 2.0 ** (-5.0 - jnp.arange(RET_HEADS, dtype=F32)))
    pos = jnp.arange(size, dtype=F32)
    rel = pos[:, None] - pos[None, :]
    decay_intra = jnp.where(rel >= 0, jnp.exp(log_gamma[:, None, None] * jnp.maximum(rel, 0.0)), 0.0)
    q_decay = jnp.exp(log_gamma[:, None] * (pos + 1.0))[:, :, None]
    k_decay = jnp.exp(log_gamma[:, None] * (size - 1.0 - pos))[:, :, None]
    chunk_decay = jnp.exp(log_gamma * size)[:, None, None]
    return decay_intra, q_decay, k_decay, chunk_decay


def retention_mixer(rq, rk, proj, gain):
    s_len = rq.shape[0]
    L = RET_CHUNK
    v_blk = _MAIN_START["rv"] // RET_W
    g_blk = _MAIN_START["rg"] // RET_W
    decay_intra, q_decay, k_decay, chunk_decay = _retention_tables(L)
    nh = RET_HEADS
    return pl.pallas_call(
        _retention_kernel,
        out_shape=jax.ShapeDtypeStruct((s_len, RET_W), BF16),
        grid=(s_len // L,),
        in_specs=[
            pl.BlockSpec((L, RET_W), lambda c: (c, 0)),
            pl.BlockSpec((L, RET_W), lambda c: (c, 0)),
            pl.BlockSpec((L, RET_W), lambda c: (c, v_blk)),
            pl.BlockSpec((L, RET_W), lambda c: (c, g_blk)),
            pl.BlockSpec((nh, L, L), lambda c: (0, 0, 0)),
            pl.BlockSpec((nh, L, 1), lambda c: (0, 0, 0)),
            pl.BlockSpec((nh, L, 1), lambda c: (0, 0, 0)),
            pl.BlockSpec((nh, 1, 1), lambda c: (0, 0, 0)),
            pl.BlockSpec((1, RET_W), lambda c: (0, 0)),
        ],
        out_specs=pl.BlockSpec((L, RET_W), lambda c: (c, 0)),
        scratch_shapes=[pltpu.VMEM((nh, RET_QK_DIM, RET_V_DIM), F32)],
        compiler_params=_cparams("arbitrary"),
        name="retention_mixer",
    )(rq, rk, proj, proj, decay_intra, q_decay, k_decay, chunk_decay, gain.reshape(1, RET_W))


def _rope_tables(s_len, d):
    half = d // 2
    inv_freq = ROPE_THETA ** (-jnp.arange(half, dtype=F32) * 2.0 / d)
    ang = jnp.arange(s_len, dtype=jnp.int32).astype(F32)[:, None] * inv_freq[None, :]
    return jnp.cos(ang)[:, None, :], jnp.sin(ang)[:, None, :]


def _rope(x, cos, sin):
    half = x.shape[-1] // 2
    x1, x2 = x[..., :half], x[..., half:]
    return jnp.concatenate([x1 * cos - x2 * sin, x2 * cos + x1 * sin], axis=-1)


def _head_rms(x, g):
    return x * lax.rsqrt(jnp.mean(x * x, axis=-1, keepdims=True) + NORM_EPS) * g


def _causal_conv(x, w):
    width = w.shape[0]
    out = x * w[width - 1]
    for j in range(width - 1):
        shift = width - 1 - j
        out = out + jnp.pad(x, ((shift, 0), (0, 0)))[:x.shape[0]] * w[j]
    return out


def _permuted_weights(w_in_layer):
    cols = lambda name: w_in_layer[:, _SEG_START[name]:_SEG_START[name] + _SEG_SIZE[name]]
    w_main = jnp.concatenate([cols(n) for n in _MAIN_ORDER], axis=1)
    tail_used = sum(_SEG_SIZE[n] for n in _TAIL_ORDER)
    w_tail = jnp.concatenate([cols(n) for n in _TAIL_ORDER]
                             + [jnp.zeros((w_in_layer.shape[0], TAIL_W - tail_used), w_in_layer.dtype)], axis=1)
    return w_main, w_tail


def _tile(n, pref):
    t = min(n, pref)
    assert n % t == 0, (n, pref)
    return t


def _layer(x, norm_mix, norm_ffn, w_in, att_q_norm, att_k_norm, mlstm_conv, mlstm_gate_bias,
           mlstm_out_norm, ret_out_norm, w_branch_att, w_branch_mlstm, w_branch_ret, w_out,
           w_ffn_in, w_ffn_out):
    s_len = x.shape[0]
    tm = _tile(s_len, 1024)
    w_main, w_tail = _permuted_weights(w_in)
    proj = norm_matmul(x, norm_mix, w_main, tm=tm, tn=512)
    tail = norm_matmul(x, norm_mix, w_tail, tm=tm, tn=TAIL_W)

    def seg(name):
        return proj[:, _MAIN_START[name]:_MAIN_START[name] + _SEG_SIZE[name]]

    def tseg(name):
        return tail[:, _TAIL_START[name]:_TAIL_START[name] + _SEG_SIZE[name]]

    cos_a, sin_a = _rope_tables(s_len, ATT_HEAD_DIM)
    cos_i, sin_i = _rope_tables(s_len, IDX_DIM)
    aq = _rope(_head_rms(seg("aq").reshape(s_len, ATT_HEADS, ATT_HEAD_DIM), att_q_norm), cos_a, sin_a)
    ak = _rope(_head_rms(seg("ak").reshape(s_len, ATT_HEADS, ATT_HEAD_DIM), att_k_norm), cos_a, sin_a)
    qt = (aq.reshape(s_len, ATT_W).T * ATT_Q_SCALE).astype(BF16)
    kk = ak.reshape(s_len, ATT_W).astype(BF16)
    vt = seg("av").T.astype(BF16)
    iq = _rope(seg("iq").reshape(s_len, IDX_HEADS, IDX_DIM), cos_i, sin_i)
    iqt = iq.reshape(s_len, IDX_HEADS * IDX_DIM).T.astype(BF16)
    ik = _rope(tseg("ik")[:, None, :], cos_i, sin_i)[:, 0, :].astype(BF16)
    wt = (tseg("iw") * (IDX_HEADS ** -0.5 * IDX_DIM ** -0.5)).T
    tq = _tile(s_len, 256)
    bias = select_bias(ik, iqt, wt, tq=tq, tk=_tile(s_len, 128))
    y_att = masked_attention(qt, kk, vt, bias, tq=tq, tk=_tile(s_len, 512))

    mqk = _causal_conv(jnp.concatenate([seg("mq"), seg("mk")], axis=-1), mlstm_conv)
    mq, mk = mqk[:, :MLSTM_QK_W], mqk[:, MLSTM_QK_W:]
    i_pre = tseg("mi") + mlstm_gate_bias[:MLSTM_HEADS]
    f_pre = tseg("mf") + mlstm_gate_bias[MLSTM_HEADS:]
    n_chunks = s_len // MLSTM_CHUNK
    logf = jax.nn.log_sigmoid(f_pre).reshape(n_chunks, MLSTM_CHUNK, MLSTM_HEADS)
    b_cum = jnp.cumsum(logf, axis=1).reshape(s_len, MLSTM_HEADS)
    gcol = jnp.concatenate([b_cum, i_pre], axis=-1)
    grow = gcol.reshape(n_chunks, MLSTM_CHUNK, 2 * MLSTM_HEADS).transpose(0, 2, 1)
    y_mlstm = mlstm_mixer(mq, mk, proj, gcol, grow, mlstm_out_norm.reshape(-1))

    rq = _rope(seg("rq").reshape(s_len, RET_HEADS, RET_QK_DIM), cos_a, sin_a).reshape(s_len, RET_W)
    rk = _rope(seg("rk").reshape(s_len, RET_HEADS, RET_QK_DIM), cos_a, sin_a).reshape(s_len, RET_W)
    y_ret = retention_mixer(rq, rk, proj, ret_out_norm.reshape(-1))

    merged = gated_merge(y_att, y_mlstm, y_ret, w_branch_att, w_branch_mlstm, w_branch_ret, proj,
                         tm=tm, tn=512)
    x = residual_matmul(x, merged, w_out, tm=tm, tn=512)

    act = norm_swiglu(x, norm_ffn, w_ffn_in, tm=tm, tn=512)
    x = residual_matmul(x, act, w_ffn_out, tm=tm, tn=256)
    return x


def kernel(x, norm_mix, norm_ffn, w_in, att_q_norm, att_k_norm, mlstm_conv, mlstm_gate_bias,
           mlstm_out_norm, ret_out_norm, w_branch_att, w_branch_mlstm, w_branch_ret, w_out,
           w_ffn_in, w_ffn_out):
    b, s_len, d = x.shape
    assert d == D_MODEL and s_len % 256 == 0
    outs = []
    for bi in range(b):
        xb = x[bi]
        for layer in range(w_in.shape[0]):
            xb = _layer(xb, norm_mix[layer], norm_ffn[layer], w_in[layer], att_q_norm[layer],
                        att_k_norm[layer], mlstm_conv[layer], mlstm_gate_bias[layer],
                        mlstm_out_norm[layer], ret_out_norm[layer], w_branch_att[layer],
                        w_branch_mlstm[layer], w_branch_ret[layer], w_out[layer], w_ffn_in[layer],
                        w_ffn_out[layer])
        outs.append(xb)
    return jnp.stack(outs, axis=0)
```

```python
import functools
import math

import jax
import jax.numpy as jnp
import numpy as np
from jax import lax
from jax.experimental import pallas as pl
from jax.experimental.pallas import tpu as pltpu

F32 = jnp.float32
BF16 = jnp.bfloat16

D_MODEL = 2048
ATT_HEADS = 8
ATT_HEAD_DIM = 128
IDX_HEADS = 8
IDX_DIM = 64
INDEX_TOPK = 256
MLSTM_HEADS = 4
MLSTM_QK_DIM = 128
MLSTM_V_DIM = 256
MLSTM_CHUNK = 256
CONV_WIDTH = 4
RET_HEADS = 8
RET_QK_DIM = 128
RET_V_DIM = 128
RET_CHUNK = 256
ROPE_THETA = 10000.0
NORM_EPS = 1e-6
FFN_DIM = 5632
N_BRANCHES = 3

ATT_W = ATT_HEADS * ATT_HEAD_DIM
MLSTM_QK_W = MLSTM_HEADS * MLSTM_QK_DIM
MLSTM_V_W = MLSTM_HEADS * MLSTM_V_DIM
RET_W = RET_HEADS * RET_V_DIM

_SEGMENTS = (
    ("aq", ATT_W), ("ak", ATT_W), ("av", ATT_W), ("iq", IDX_HEADS * IDX_DIM), ("ik", IDX_DIM),
    ("iw", IDX_HEADS), ("mq", MLSTM_QK_W), ("mk", MLSTM_QK_W), ("mv", MLSTM_V_W),
    ("mi", MLSTM_HEADS), ("mf", MLSTM_HEADS), ("mo", MLSTM_V_W), ("rq", RET_W), ("rk", RET_W),
    ("rv", RET_W), ("rg", RET_W), ("gates", N_BRANCHES * D_MODEL),
)
_SEG_SIZE = dict(_SEGMENTS)
_SEG_START = {}
_off = 0
for _name, _size in _SEGMENTS:
    _SEG_START[_name] = _off
    _off += _size

_MAIN_ORDER = ("aq", "ak", "av", "mv", "mo", "rq", "rk", "rv", "rg", "gates", "iq", "mq", "mk")
_TAIL_ORDER = ("ik", "iw", "mi", "mf")
_MAIN_START = {}
_off = 0
for _name in _MAIN_ORDER:
    _MAIN_START[_name] = _off
    _off += _SEG_SIZE[_name]
MAIN_W = _off
_TAIL_START = {}
_off = 0
for _name in _TAIL_ORDER:
    _TAIL_START[_name] = _off
    _off += _SEG_SIZE[_name]
LANES = 128
TAIL_W = LANES

VMEM_LIMIT = 56 * 1024 * 1024
MASK_VALUE = -1e30


def _cparams(*semantics):
    return pltpu.CompilerParams(dimension_semantics=semantics, vmem_limit_bytes=VMEM_LIMIT)


def _rms_rows(x, g):
    ms = jnp.mean(x * x, axis=-1, keepdims=True)
    return x * lax.rsqrt(ms + NORM_EPS) * g


def _rms_norm_kernel(x_ref, g_ref, o_ref):
    o_ref[...] = _rms_rows(x_ref[...], g_ref[...]).astype(o_ref.dtype)


def rms_norm_bf16(x, g, *, tm):
    m, k = x.shape
    return pl.pallas_call(
        _rms_norm_kernel,
        out_shape=jax.ShapeDtypeStruct((m, k), BF16),
        grid=(m // tm,),
        in_specs=[pl.BlockSpec((tm, k), lambda i: (i, 0)), pl.BlockSpec((1, k), lambda i: (0, 0))],
        out_specs=pl.BlockSpec((tm, k), lambda i: (i, 0)),
        compiler_params=_cparams("parallel"),
        name="rms_norm",
    )(x, g.reshape(1, k))


def _matmul_kernel(a_ref, w_ref, o_ref):
    o_ref[...] = jnp.dot(a_ref[...], w_ref[...].astype(BF16), preferred_element_type=F32)


def matmul(a, w, *, tm, tn):
    m, k = a.shape
    n = w.shape[1]
    return pl.pallas_call(
        _matmul_kernel,
        out_shape=jax.ShapeDtypeStruct((m, n), F32),
        grid=(m // tm, n // tn),
        in_specs=[
            pl.BlockSpec((tm, k), lambda i, j: (i, 0)),
            pl.BlockSpec((k, tn), lambda i, j: (0, j)),
        ],
        out_specs=pl.BlockSpec((tm, tn), lambda i, j: (i, j)),
        compiler_params=_cparams("parallel", "arbitrary"),
        name="matmul",
    )(a, w)


def _norm_swiglu_kernel(x_ref, g_ref, wg_ref, wu_ref, o_ref, h_ref):
    @pl.when(pl.program_id(1) == 0)
    def _():
        h_ref[...] = _rms_rows(x_ref[...], g_ref[...]).astype(BF16)

    h = h_ref[...]
    gate = jnp.dot(h, wg_ref[...].astype(BF16), preferred_element_type=F32)
    up = jnp.dot(h, wu_ref[...].astype(BF16), preferred_element_type=F32)
    o_ref[...] = (gate * jax.nn.sigmoid(gate) * up).astype(o_ref.dtype)


def norm_swiglu(x, g, w_in, *, tm, tn):
    m, k = x.shape
    f = w_in.shape[1] // 2
    nj = f // tn
    return pl.pallas_call(
        _norm_swiglu_kernel,
        out_shape=jax.ShapeDtypeStruct((m, f), BF16),
        grid=(m // tm, nj),
        in_specs=[
            pl.BlockSpec((tm, k), lambda i, j: (i, 0)),
            pl.BlockSpec((1, k), lambda i, j: (0, 0)),
            pl.BlockSpec((k, tn), lambda i, j: (0, j)),
            pl.BlockSpec((k, tn), lambda i, j: (0, j + nj)),
        ],
        out_specs=pl.BlockSpec((tm, tn), lambda i, j: (i, j)),
        scratch_shapes=[pltpu.VMEM((tm, k), BF16)],
        compiler_params=_cparams("parallel", "arbitrary"),
        name="norm_swiglu",
    )(x, g.reshape(1, k), w_in, w_in)


def _residual_matmul_kernel(r_ref, a_ref, w_ref, o_ref):
    o_ref[...] = r_ref[...] + jnp.dot(a_ref[...], w_ref[...].astype(BF16), preferred_element_type=F32)


def residual_matmul(res, a, w, *, tm, tn):
    m, k = a.shape
    n = w.shape[1]
    return pl.pallas_call(
        _residual_matmul_kernel,
        out_shape=jax.ShapeDtypeStruct((m, n), F32),
        grid=(m // tm, n // tn),
        in_specs=[
            pl.BlockSpec((tm, tn), lambda i, j: (i, j)),
            pl.BlockSpec((tm, k), lambda i, j: (i, 0)),
            pl.BlockSpec((k, tn), lambda i, j: (0, j)),
        ],
        out_specs=pl.BlockSpec((tm, tn), lambda i, j: (i, j)),
        compiler_params=_cparams("parallel", "arbitrary"),
        name="residual_matmul",
    )(res, a, w)


def _merge_kernel(ya_ref, ym_ref, yr_ref, wa_ref, wm_ref, wr_ref, ga_ref, gm_ref, gr_ref, o_ref):
    def branch(y_ref, w_ref, g_ref):
        proj = jnp.dot(y_ref[...], w_ref[...].astype(BF16), preferred_element_type=F32)
        return jax.nn.sigmoid(g_ref[...]) * proj

    merged = branch(ya_ref, wa_ref, ga_ref) + branch(ym_ref, wm_ref, gm_ref) + branch(yr_ref, wr_ref, gr_ref)
    o_ref[...] = merged.astype(o_ref.dtype)


def gated_merge(y_att, y_mlstm, y_ret, w_att, w_mlstm, w_ret, proj, *, tm, tn):
    m, k = y_att.shape
    n = w_att.shape[1]
    gate_blk = _MAIN_START["gates"] // tn
    per_branch = n // tn
    y_spec = pl.BlockSpec((tm, k), lambda i, j: (i, 0))
    w_spec = pl.BlockSpec((k, tn), lambda i, j: (0, j))

    def g_spec(b):
        return pl.BlockSpec((tm, tn), lambda i, j: (i, gate_blk + b * per_branch + j))

    return pl.pallas_call(
        _merge_kernel,
        out_shape=jax.ShapeDtypeStruct((m, n), BF16),
        grid=(m // tm, n // tn),
        in_specs=[y_spec, y_spec, y_spec, w_spec, w_spec, w_spec, g_spec(0), g_spec(1), g_spec(2)],
        out_specs=pl.BlockSpec((tm, tn), lambda i, j: (i, j)),
        compiler_params=_cparams("parallel", "arbitrary"),
        name="gated_merge",
    )(y_att, y_mlstm, y_ret, w_att, w_mlstm, w_ret, proj, proj, proj)


INT_MIN = -(2 ** 31)


def _sortable(x):
    bits = pltpu.bitcast(x, jnp.int32)
    bits = jnp.where(bits == INT_MIN, 0, bits)
    return bits ^ ((bits >> 31) & 0x7FFFFFFF)


GROUP = 256


def _bit_transpose32(words):
    a = list(words)
    j, m = 16, 0x0000FFFF
    while j:
        k = 0
        while k < 32:
            t = (a[k] ^ lax.shift_right_logical(a[k + j], jnp.int32(j))) & m
            a[k] = a[k] ^ t
            a[k + j] = a[k + j] ^ (t << j)
            k = (k + j + 1) & ~j
        j >>= 1
        m = (m ^ (m << j)) & 0xFFFFFFFF
    return a


def _select_kernel(ik_ref, iqt_ref, wt_ref, bias_ref, key_ref, plane_ref, alive_ref, *, tq, tk, topk):
    s_len = ik_ref.shape[0]
    i = pl.program_id(0)
    q_lo = i * tq
    n_groups = (q_lo + tq + GROUP - 1) // GROUP
    n_chunks = n_groups * (GROUP // tk)
    q_pos = q_lo + lax.broadcasted_iota(jnp.int32, (tk, tq), 1)
    row_iota = lax.broadcasted_iota(jnp.int32, (tk, tq), 0)

    @pl.when(i == 0)
    def _():
        plane_ref[...] = jnp.zeros(plane_ref.shape, jnp.int32)

    def score_group(g, carry):
        g0 = pl.multiple_of(g * GROUP, GROUP)
        slabs = []
        for sub in range(GROUP // tk):
            k0 = g0 + sub * tk
            ik = ik_ref[pl.ds(k0, tk), :]
            acc = jnp.zeros((tk, tq), F32)
            for h in range(IDX_HEADS):
                lg = jnp.dot(ik, iqt_ref[h * IDX_DIM:(h + 1) * IDX_DIM, :], preferred_element_type=F32)
                acc = acc + wt_ref[h:h + 1, :] * jnp.maximum(lg, 0.0)
            key = _sortable(acc)
            key = jnp.where(k0 + row_iota <= q_pos, key, INT_MIN)
            key_ref[pl.ds(k0, tk), :] = key
            slabs += [key[8 * j:8 * j + 8, :] for j in range(tk // 8)]
        planes = _bit_transpose32(slabs)
        r0 = pl.multiple_of(g * 8, 8)
        for p in range(32):
            word = planes[31 - p]
            plane_ref[p, pl.ds(r0, 8), :] = ~word if p == 31 else word
        return carry

    lax.fori_loop(0, n_groups, score_group, 0)

    grp = lax.broadcasted_iota(jnp.int32, alive_ref.shape, 0) >> 3
    alive_ref[...] = jnp.where(grp < n_groups, -1, 0)
    n_gt = jnp.zeros((1, tq), jnp.int32)
    thr_u = jnp.zeros((1, tq), jnp.int32)
    for p in range(31, -1, -1):
        alive = alive_ref[...]
        ones = alive & plane_ref[p]
        c = jnp.sum(lax.population_count(ones), axis=0, keepdims=True)
        take1 = (n_gt + c) >= topk
        alive_ref[...] = jnp.where(take1, ones, alive ^ ones)
        n_gt = n_gt + jnp.where(take1, 0, c)
        thr_u = thr_u | jnp.where(take1, INT_MIN if p == 31 else (1 << p), 0)
    thr = thr_u ^ INT_MIN
    n_eq = jnp.sum(lax.population_count(alive_ref[...]), axis=0, keepdims=True)

    need = topk - n_gt
    excess = jnp.max(n_eq - need)

    def count(pred):
        def body(c, acc):
            k0 = pl.multiple_of(c * tk, tk)
            hit = pred(key_ref[pl.ds(k0, tk), :], k0).astype(jnp.int32)
            return acc + jnp.sum(hit.reshape(tk // 8, 8, tq), axis=0)

        acc = lax.fori_loop(0, n_chunks, body, jnp.zeros((8, tq), jnp.int32))
        return jnp.sum(acc, axis=0, keepdims=True)

    def tie_cut():
        cut = jnp.zeros((1, tq), jnp.int32)
        n_bits = max(1, int(math.ceil(math.log2(s_len))))
        for bit in range(n_bits - 1, -1, -1):
            cand = cut | (1 << bit)
            n_before = count(lambda key, k0, cand=cand: (key == thr) & (k0 + row_iota < cand))
            cut = jnp.where(n_before < need, cand, cut)
        return cut

    cut = lax.cond(excess > 0, tie_cut, lambda: jnp.full((1, tq), s_len, jnp.int32))

    def bias_chunk(c, carry):
        k0 = pl.multiple_of(c * tk, tk)
        key = key_ref[pl.ds(k0, tk), :]
        k_pos = k0 + row_iota
        sel = (key > thr) | ((key == thr) & (k_pos <= cut))
        sel = sel & (k_pos <= q_pos)
        bias_ref[pl.ds(k0, tk), :] = jnp.where(sel, 0.0, MASK_VALUE).astype(bias_ref.dtype)
        return carry

    lax.fori_loop(0, n_chunks, bias_chunk, 0)

    def fill_chunk(c, carry):
        k0 = pl.multiple_of(c * tk, tk)
        bias_ref[pl.ds(k0, tk), :] = jnp.full((tk, tq), MASK_VALUE, bias_ref.dtype)
        return carry

    lax.fori_loop(n_chunks, s_len // tk, fill_chunk, 0)


def select_bias(ik, iqt, wt, *, tq, tk):
    s_len = ik.shape[0]
    topk = min(INDEX_TOPK, s_len // 4)
    kern = functools.partial(_select_kernel, tq=tq, tk=tk, topk=topk)
    return pl.pallas_call(
        kern,
        out_shape=jax.ShapeDtypeStruct((s_len, s_len), BF16),
        grid=(s_len // tq,),
        in_specs=[
            pl.BlockSpec((s_len, IDX_DIM), lambda i: (0, 0)),
            pl.BlockSpec((IDX_HEADS * IDX_DIM, tq), lambda i: (0, i)),
            pl.BlockSpec((IDX_HEADS, tq), lambda i: (0, i)),
        ],
        out_specs=pl.BlockSpec((s_len, tq), lambda i: (0, i)),
        scratch_shapes=[
            pltpu.VMEM((s_len, tq), jnp.int32),
            pltpu.VMEM((32, s_len // 32, tq), jnp.int32),
            pltpu.VMEM((s_len // 32, tq), jnp.int32),
        ],
        compiler_params=_cparams("arbitrary"),
        name="select_bias",
    )(ik, iqt, wt)


ATT_Q_SCALE = ATT_HEAD_DIM ** -0.5 * math.log2(math.e)


def _attention_kernel(qi_ref, ki_ref, qt_ref, k_ref, vt_ref, bias_ref, o_ref, m_ref, l_ref, acc_ref,
                      s_ref, *, tq, tk):
    step = pl.program_id(0)
    i = qi_ref[step]
    kb = ki_ref[step]
    last_kb = (i * tq + tq - 1) // tk
    dh = ATT_HEAD_DIM

    @pl.when(kb == 0)
    def _():
        m_ref[...] = jnp.full(m_ref.shape, MASK_VALUE, F32)
        l_ref[...] = jnp.zeros(l_ref.shape, F32)
        acc_ref[...] = jnp.zeros(acc_ref.shape, F32)

    bias = bias_ref[...].astype(F32)

    m_new = []
    for h in range(ATT_HEADS):
        hs = slice(h * dh, (h + 1) * dh)
        s = jnp.dot(k_ref[:, hs], qt_ref[hs, :], preferred_element_type=F32) + bias
        s_ref[h] = s
        m_new.append(jnp.maximum(m_ref[h], jnp.max(s, axis=0, keepdims=True)))
    for h in range(ATT_HEADS):
        hs = slice(h * dh, (h + 1) * dh)
        alpha = jnp.exp2(m_ref[h] - m_new[h])
        p = jnp.exp2(s_ref[h] - m_new[h])
        l_ref[h] = alpha * l_ref[h] + jnp.sum(p, axis=0, keepdims=True)
        acc_ref[h] = alpha * acc_ref[h] + jnp.dot(vt_ref[hs, :], p.astype(BF16),
                                                  preferred_element_type=F32)
        m_ref[h] = m_new[h]

    @pl.when(kb == last_kb)
    def _():
        for h in range(ATT_HEADS):
            out_t = acc_ref[h] / l_ref[h]
            o_ref[:, h * dh:(h + 1) * dh] = out_t.T.astype(o_ref.dtype)


def masked_attention(qt, k, vt, bias, *, tq, tk):
    width, s_len = qt.shape
    pairs = [(i, kb) for i in range(s_len // tq) for kb in range((i * tq + tq - 1) // tk + 1)]
    qi = jnp.asarray([p[0] for p in pairs], jnp.int32)
    ki = jnp.asarray([p[1] for p in pairs], jnp.int32)
    grid_spec = pltpu.PrefetchScalarGridSpec(
        num_scalar_prefetch=2,
        grid=(len(pairs),),
        in_specs=[
            pl.BlockSpec((width, tq), lambda s, qi, ki: (0, qi[s])),
            pl.BlockSpec((tk, width), lambda s, qi, ki: (ki[s], 0)),
            pl.BlockSpec((width, tk), lambda s, qi, ki: (0, ki[s])),
            pl.BlockSpec((tk, tq), lambda s, qi, ki: (ki[s], qi[s])),
        ],
        out_specs=pl.BlockSpec((tq, width), lambda s, qi, ki: (qi[s], 0)),
        scratch_shapes=[
            pltpu.VMEM((ATT_HEADS, 1, tq), F32),
            pltpu.VMEM((ATT_HEADS, 1, tq), F32),
            pltpu.VMEM((ATT_HEADS, ATT_HEAD_DIM, tq), F32),
            pltpu.VMEM((ATT_HEADS, tk, tq), F32),
        ],
    )
    return pl.pallas_call(
        functools.partial(_attention_kernel, tq=tq, tk=tk),
        out_shape=jax.ShapeDtypeStruct((s_len, width), BF16),
        grid_spec=grid_spec,
        compiler_params=_cparams("arbitrary"),
        name="masked_attention",
    )(qi, ki, qt, k, vt, bias)


HALO = 8


def _mlstm_kernel(q_ref, k_ref, v_ref, o_ref, gcol_ref, grow_ref, gain_ref, cw_ref, y_ref,
                  c_ref, n_ref, m_ref, halo_ref, qk_ref, qc_ref):
    L = q_ref.shape[0]
    dk, dv, nh = MLSTM_QK_DIM, MLSTM_V_DIM, MLSTM_HEADS
    qk_w = nh * dk

    @pl.when(pl.program_id(0) == 0)
    def _():
        c_ref[...] = jnp.zeros(c_ref.shape, F32)
        n_ref[...] = jnp.zeros(n_ref.shape, F32)
        m_ref[...] = jnp.zeros(m_ref.shape, F32)
        halo_ref[...] = jnp.zeros(halo_ref.shape, F32)

    def causal_conv(x_ref, slot, taps):
        x = x_ref[...]
        ext = jnp.concatenate([halo_ref[slot], x], axis=0)
        out = x * taps[CONV_WIDTH - 1:CONV_WIDTH, :]
        for j in range(CONV_WIDTH - 1):
            lo = HALO - (CONV_WIDTH - 1 - j)
            out = out + ext[lo:lo + L, :] * taps[j:j + 1, :]
        halo_ref[slot] = x[L - HALO:, :]
        return out

    q_all = causal_conv(q_ref, 0, cw_ref[:, :qk_w])
    k_all = causal_conv(k_ref, 1, cw_ref[:, qk_w:])

    tril = lax.broadcasted_iota(jnp.int32, (L, L), 1) <= lax.broadcasted_iota(jnp.int32, (L, L), 0)
    gcol = gcol_ref[...]
    grow = grow_ref[0]
    stats = []
    for h in range(nh):
        q = q_all[:, h * dk:(h + 1) * dk]
        k = k_all[:, h * dk:(h + 1) * dk] * (dk ** -0.5)
        b_col = gcol[:, h:h + 1]
        i_col = gcol[:, nh + h:nh + h + 1]
        b_row = grow[h:h + 1, :]
        i_row = grow[nh + h:nh + h + 1, :]
        m_prev = m_ref[h]
        c_prev = c_ref[h]
        n_prev = n_ref[h]
        qb = q.astype(BF16)
        vb = v_ref[:, h * dv:(h + 1) * dv].astype(BF16)
        qk_ref[h] = lax.dot_general(qb, k.astype(BF16), (((1,), (1,)), ((), ())),
                                    preferred_element_type=F32)
        qc_ref[h] = jnp.dot(qb, c_prev.astype(BF16), preferred_element_type=F32)
        qn = jnp.sum(q * n_prev, axis=-1, keepdims=True)

        b_end = b_row[:, L - 1:L]
        end_row = b_end - b_row + i_row
        m_new = jnp.maximum(b_end + m_prev, jnp.max(end_row, axis=-1, keepdims=True))
        decay = jnp.exp(b_end + m_prev - m_new)
        end_col = b_end - b_col + i_col
        ka = k * jnp.exp(end_col - m_new)
        c_ref[h] = decay * c_prev + jnp.dot(ka.T.astype(BF16), vb, preferred_element_type=F32)
        n_ref[h] = decay * n_prev + jnp.sum(ka, axis=0, keepdims=True)
        m_ref[h] = m_new
        stats.append((b_col, b_row, i_row, m_prev, qn))

    for h in range(nh):
        b_col, b_row, i_row, m_prev, qn = stats[h]
        vb = v_ref[:, h * dv:(h + 1) * dv].astype(BF16)
        log_intra = jnp.where(tril, b_col - b_row + i_row, -jnp.inf)
        m_inter = b_col + m_prev
        m_t = jnp.maximum(m_inter, jnp.max(log_intra, axis=-1, keepdims=True))
        w_inter = jnp.exp(m_inter - m_t)
        p = jnp.exp(log_intra - m_t) * qk_ref[h]
        num = w_inter * qc_ref[h] + jnp.dot(p.astype(BF16), vb, preferred_element_type=F32)
        den = w_inter * qn + jnp.sum(p, axis=-1, keepdims=True)
        h_out = num / jnp.maximum(jnp.abs(den), jnp.exp(-m_t))

        gain = gain_ref[:, h * dv:(h + 1) * dv]
        normed = h_out * lax.rsqrt(jnp.mean(h_out * h_out, axis=-1, keepdims=True) + NORM_EPS) * gain
        y_ref[:, h * dv:(h + 1) * dv] = (normed * jax.nn.sigmoid(o_ref[:, h * dv:(h + 1) * dv])).astype(y_ref.dtype)


def mlstm_mixer(proj, gcol, grow, gain, conv_w):
    s_len = proj.shape[0]
    L = MLSTM_CHUNK
    q_blk = _MAIN_START["mq"] // MLSTM_QK_W
    k_blk = _MAIN_START["mk"] // MLSTM_QK_W
    v_blk = _MAIN_START["mv"] // MLSTM_V_W
    o_blk = _MAIN_START["mo"] // MLSTM_V_W
    return pl.pallas_call(
        _mlstm_kernel,
        out_shape=jax.ShapeDtypeStruct((s_len, MLSTM_V_W), BF16),
        grid=(s_len // L,),
        in_specs=[
            pl.BlockSpec((L, MLSTM_QK_W), lambda c: (c, q_blk)),
            pl.BlockSpec((L, MLSTM_QK_W), lambda c: (c, k_blk)),
            pl.BlockSpec((L, MLSTM_V_W), lambda c: (c, v_blk)),
            pl.BlockSpec((L, MLSTM_V_W), lambda c: (c, o_blk)),
            pl.BlockSpec((L, 2 * MLSTM_HEADS), lambda c: (c, 0)),
            pl.BlockSpec((1, 2 * MLSTM_HEADS, L), lambda c: (c, 0, 0)),
            pl.BlockSpec((1, MLSTM_V_W), lambda c: (0, 0)),
            pl.BlockSpec((CONV_WIDTH, 2 * MLSTM_QK_W), lambda c: (0, 0)),
        ],
        out_specs=pl.BlockSpec((L, MLSTM_V_W), lambda c: (c, 0)),
        scratch_shapes=[
            pltpu.VMEM((MLSTM_HEADS, MLSTM_QK_DIM, MLSTM_V_DIM), F32),
            pltpu.VMEM((MLSTM_HEADS, 1, MLSTM_QK_DIM), F32),
            pltpu.VMEM((MLSTM_HEADS, 1, 1), F32),
            pltpu.VMEM((2, HALO, MLSTM_QK_W), F32),
            pltpu.VMEM((MLSTM_HEADS, L, L), F32),
            pltpu.VMEM((MLSTM_HEADS, L, MLSTM_V_DIM), F32),
        ],
        compiler_params=_cparams("arbitrary"),
        name="mlstm_mixer",
    )(proj, proj, proj, proj, gcol, grow, gain.reshape(1, MLSTM_V_W), conv_w)


def _rope_full(x, cos2, sin2):
    return x * cos2 + pltpu.roll(x, x.shape[-1] // 2, 1) * sin2


def _retention_kernel(q_ref, k_ref, v_ref, g_ref, cos_ref, sin_ref, dint_ref, qdec_ref, kdec_ref,
                      cdec_ref, gain_ref, y_ref, r_ref, qk_ref, cross_ref):
    dk, dv = RET_QK_DIM, RET_V_DIM

    @pl.when(pl.program_id(0) == 0)
    def _():
        r_ref[...] = jnp.zeros(r_ref.shape, F32)

    cos2 = cos_ref[...]
    sin2 = sin_ref[...]
    for h in range(RET_HEADS):
        q = _rope_full(q_ref[:, h * dk:(h + 1) * dk], cos2, sin2)
        k = _rope_full(k_ref[:, h * dk:(h + 1) * dk], cos2, sin2) * (dk ** -0.5)
        vb = v_ref[:, h * dv:(h + 1) * dv].astype(BF16)
        r_prev = r_ref[h]
        qk_ref[h] = lax.dot_general(q.astype(BF16), k.astype(BF16), (((1,), (1,)), ((), ())),
                                    preferred_element_type=F32)
        cross_ref[h] = jnp.dot((q * qdec_ref[h]).astype(BF16), r_prev.astype(BF16),
                               preferred_element_type=F32)
        kd = k * kdec_ref[h]
        r_ref[h] = cdec_ref[h] * r_prev + jnp.dot(kd.T.astype(BF16), vb, preferred_element_type=F32)

    for h in range(RET_HEADS):
        vb = v_ref[:, h * dv:(h + 1) * dv].astype(BF16)
        inner = jnp.dot((qk_ref[h] * dint_ref[h]).astype(BF16), vb, preferred_element_type=F32)
        out = inner + cross_ref[h]

        mu = jnp.mean(out, axis=-1, keepdims=True)
        xc = out - mu
        normed = xc * lax.rsqrt(jnp.mean(xc * xc, axis=-1, keepdims=True) + NORM_EPS)
        gate = g_ref[:, h * dv:(h + 1) * dv]
        y = normed * gain_ref[:, h * dv:(h + 1) * dv] * (gate * jax.nn.sigmoid(gate))
        y_ref[:, h * dv:(h + 1) * dv] = y.astype(y_ref.dtype)


def _retention_tables(size):
    log_gamma = jnp.log(1.0 - 2.0 ** (-5.0 - jnp.arange(RET_HEADS, dtype=F32)))
    pos = jnp.arange(size, dtype=F32)
    rel = pos[:, None] - pos[None, :]
    decay_intra = jnp.where(rel >= 0, jnp.exp(log_gamma[:, None, None] * jnp.maximum(rel, 0.0)), 0.0)
    q_decay = jnp.exp(log_gamma[:, None] * (pos + 1.0))[:, :, None]
    k_decay = jnp.exp(log_gamma[:, None] * (size - 1.0 - pos))[:, :, None]
    chunk_decay = jnp.exp(log_gamma * size)[:, None, None]
    return decay_intra, q_decay, k_decay, chunk_decay


def retention_mixer(proj, cos2, sin2, gain):
    s_len = proj.shape[0]
    L = RET_CHUNK
    q_blk = _MAIN_START["rq"] // RET_W
    k_blk = _MAIN_START["rk"] // RET_W
    v_blk = _MAIN_START["rv"] // RET_W
    g_blk = _MAIN_START["rg"] // RET_W
    decay_intra, q_decay, k_decay, chunk_decay = _retention_tables(L)
    nh = RET_HEADS
    return pl.pallas_call(
        _retention_kernel,
        out_shape=jax.ShapeDtypeStruct((s_len, RET_W), BF16),
        grid=(s_len // L,),
        in_specs=[
            pl.BlockSpec((L, RET_W), lambda c: (c, q_blk)),
            pl.BlockSpec((L, RET_W), lambda c: (c, k_blk)),
            pl.BlockSpec((L, RET_W), lambda c: (c, v_blk)),
            pl.BlockSpec((L, RET_W), lambda c: (c, g_blk)),
            pl.BlockSpec((L, RET_QK_DIM), lambda c: (c, 0)),
            pl.BlockSpec((L, RET_QK_DIM), lambda c: (c, 0)),
            pl.BlockSpec((nh, L, L), lambda c: (0, 0, 0)),
            pl.BlockSpec((nh, L, 1), lambda c: (0, 0, 0)),
            pl.BlockSpec((nh, L, 1), lambda c: (0, 0, 0)),
            pl.BlockSpec((nh, 1, 1), lambda c: (0, 0, 0)),
            pl.BlockSpec((1, RET_W), lambda c: (0, 0)),
        ],
        out_specs=pl.BlockSpec((L, RET_W), lambda c: (c, 0)),
        scratch_shapes=[
            pltpu.VMEM((nh, RET_QK_DIM, RET_V_DIM), F32),
            pltpu.VMEM((nh, L, L), F32),
            pltpu.VMEM((nh, L, RET_V_DIM), F32),
        ],
        compiler_params=_cparams("arbitrary"),
        name="retention_mixer",
    )(proj, proj, proj, proj, cos2, sin2, decay_intra, q_decay, k_decay, chunk_decay,
      gain.reshape(1, RET_W))


def _rope_tables(s_len, d):
    half = d // 2
    inv_freq = ROPE_THETA ** (-jnp.arange(half, dtype=F32) * 2.0 / d)
    ang = jnp.arange(s_len, dtype=jnp.int32).astype(F32)[:, None] * inv_freq[None, :]
    cos, sin = jnp.cos(ang), jnp.sin(ang)
    reps = LANES // d
    return jnp.tile(jnp.concatenate([cos, cos], -1), (1, reps)), jnp.tile(jnp.concatenate([-sin, sin], -1), (1, reps))


def _rope_pair(x, cos4, sin4):
    half = IDX_DIM // 2
    lane = lax.broadcasted_iota(jnp.int32, x.shape, 1)
    partner = jnp.where((lane % IDX_DIM) < half, pltpu.roll(x, LANES - half, 1), pltpu.roll(x, half, 1))
    return x * cos4 + partner * sin4


def _att_prep_kernel(aq_ref, ak_ref, av_ref, iq_ref, tail_ref, cosa_ref, sina_ref, cosi_ref, sini_ref,
                     gq_ref, gk_ref, qt_ref, k_ref, vt_ref, iqt_ref, ik_ref, wt_ref):
    dh = ATT_HEAD_DIM
    cos_a, sin_a = cosa_ref[...], sina_ref[...]
    cos_i, sin_i = cosi_ref[...], sini_ref[...]
    for h in range(ATT_HEADS):
        hs = slice(h * dh, (h + 1) * dh)
        q = _rope_full(_rms_rows(aq_ref[:, hs], gq_ref[...]), cos_a, sin_a)
        qt_ref[hs, :] = (q * ATT_Q_SCALE).T.astype(qt_ref.dtype)
        k = _rope_full(_rms_rows(ak_ref[:, hs], gk_ref[...]), cos_a, sin_a)
        k_ref[:, hs] = k.astype(k_ref.dtype)
        vt_ref[hs, :] = av_ref[:, hs].T.astype(vt_ref.dtype)
    for c in range(IDX_HEADS * IDX_DIM // LANES):
        cs = slice(c * LANES, (c + 1) * LANES)
        iqt_ref[cs, :] = _rope_pair(iq_ref[:, cs], cos_i, sin_i).T.astype(iqt_ref.dtype)
    tail = tail_ref[...]
    ik_lo = _TAIL_START["ik"]
    ik_ref[...] = _rope_pair(tail, cos_i, sin_i)[:, ik_lo:ik_lo + IDX_DIM].astype(ik_ref.dtype)
    iw_lo = _TAIL_START["iw"]
    wt_ref[...] = (tail * (IDX_HEADS ** -0.5 * IDX_DIM ** -0.5)).T[iw_lo:iw_lo + IDX_HEADS, :]


def attention_prep(proj, tail, rope_a, rope_i, gq, gk, *, tm):
    s_len = proj.shape[0]
    iq_w = IDX_HEADS * IDX_DIM
    row = lambda width, blk: pl.BlockSpec((tm, width), lambda i: (i, blk))
    col = lambda height: pl.BlockSpec((height, tm), lambda i: (0, i))
    gain = pl.BlockSpec((1, ATT_HEAD_DIM), lambda i: (0, 0))
    return pl.pallas_call(
        _att_prep_kernel,
        out_shape=[
            jax.ShapeDtypeStruct((ATT_W, s_len), BF16),
            jax.ShapeDtypeStruct((s_len, ATT_W), BF16),
            jax.ShapeDtypeStruct((ATT_W, s_len), BF16),
            jax.ShapeDtypeStruct((iq_w, s_len), BF16),
            jax.ShapeDtypeStruct((s_len, IDX_DIM), BF16),
            jax.ShapeDtypeStruct((IDX_HEADS, s_len), F32),
        ],
        grid=(s_len // tm,),
        in_specs=[
            row(ATT_W, _MAIN_START["aq"] // ATT_W), row(ATT_W, _MAIN_START["ak"] // ATT_W),
            row(ATT_W, _MAIN_START["av"] // ATT_W), row(iq_w, _MAIN_START["iq"] // iq_w),
            row(TAIL_W, 0), row(LANES, 0), row(LANES, 0), row(LANES, 0), row(LANES, 0), gain, gain,
        ],
        out_specs=[col(ATT_W), row(ATT_W, 0), col(ATT_W), col(iq_w), row(IDX_DIM, 0), col(IDX_HEADS)],
        compiler_params=_cparams("parallel"),
        name="attention_prep",
    )(proj, proj, proj, proj, tail, rope_a[0], rope_a[1], rope_i[0], rope_i[1],
      gq.reshape(1, ATT_HEAD_DIM), gk.reshape(1, ATT_HEAD_DIM))


def _permuted_weights(w_in_layer):
    cols = lambda name: w_in_layer[:, _SEG_START[name]:_SEG_START[name] + _SEG_SIZE[name]]
    w_main = jnp.concatenate([cols(n) for n in _MAIN_ORDER], axis=1)
    tail_used = sum(_SEG_SIZE[n] for n in _TAIL_ORDER)
    w_tail = jnp.concatenate([cols(n) for n in _TAIL_ORDER]
                             + [jnp.zeros((w_in_layer.shape[0], TAIL_W - tail_used), w_in_layer.dtype)], axis=1)
    return w_main, w_tail


def _tile(n, pref):
    t = min(n, pref)
    assert n % t == 0, (n, pref)
    return t


def _layer(x, rope_a, rope_i, norm_mix, norm_ffn, w_in, att_q_norm, att_k_norm, mlstm_conv,
           mlstm_gate_bias, mlstm_out_norm, ret_out_norm, w_branch_att, w_branch_mlstm, w_branch_ret,
           w_out, w_ffn_in, w_ffn_out):
    s_len = x.shape[0]
    tm = _tile(s_len, 1024)
    w_main, w_tail = _permuted_weights(w_in)
    h = rms_norm_bf16(x, norm_mix, tm=_tile(s_len, 512))
    proj = matmul(h, w_main, tm=_tile(s_len, 2048), tn=512)
    tail = matmul(h, w_tail, tm=_tile(s_len, 2048), tn=TAIL_W)

    def tseg(name):
        return tail[:, _TAIL_START[name]:_TAIL_START[name] + _SEG_SIZE[name]]

    tq = _tile(s_len, 256)
    qt, kk, vt, iqt, ik, wt = attention_prep(proj, tail, rope_a, rope_i, att_q_norm, att_k_norm, tm=tq)
    bias = select_bias(ik, iqt, wt, tq=tq, tk=_tile(s_len, 128))
    y_att = masked_attention(qt, kk, vt, bias, tq=tq, tk=_tile(s_len, 512))

    i_pre = tseg("mi") + mlstm_gate_bias[:MLSTM_HEADS]
    f_pre = tseg("mf") + mlstm_gate_bias[MLSTM_HEADS:]
    n_chunks = s_len // MLSTM_CHUNK
    logf = jax.nn.log_sigmoid(f_pre).reshape(n_chunks, MLSTM_CHUNK, MLSTM_HEADS)
    b_cum = jnp.cumsum(logf, axis=1).reshape(s_len, MLSTM_HEADS)
    gcol = jnp.concatenate([b_cum, i_pre], axis=-1)
    grow = gcol.reshape(n_chunks, MLSTM_CHUNK, 2 * MLSTM_HEADS).transpose(0, 2, 1)
    y_mlstm = mlstm_mixer(proj, gcol, grow, mlstm_out_norm.reshape(-1), mlstm_conv)

    y_ret = retention_mixer(proj, rope_a[0], rope_a[1], ret_out_norm.reshape(-1))

    merged = gated_merge(y_att, y_mlstm, y_ret, w_branch_att, w_branch_mlstm, w_branch_ret, proj,
                         tm=tm, tn=512)
    x = residual_matmul(x, merged, w_out, tm=tm, tn=512)

    act = norm_swiglu(x, norm_ffn, w_ffn_in, tm=tm, tn=512)
    x = residual_matmul(x, act, w_ffn_out, tm=tm, tn=256)
    return x


def kernel(x, norm_mix, norm_ffn, w_in, att_q_norm, att_k_norm, mlstm_conv, mlstm_gate_bias,
           mlstm_out_norm, ret_out_norm, w_branch_att, w_branch_mlstm, w_branch_ret, w_out,
           w_ffn_in, w_ffn_out):
    b, s_len, d = x.shape
    assert d == D_MODEL and s_len % 256 == 0
    rope_a = _rope_tables(s_len, ATT_HEAD_DIM)
    rope_i = _rope_tables(s_len, IDX_DIM)
    outs = []
    for bi in range(b):
        xb = x[bi]
        for layer in range(w_in.shape[0]):
            xb = _layer(xb, rope_a, rope_i, norm_mix[layer], norm_ffn[layer], w_in[layer], att_q_norm[layer],
                        att_k_norm[layer], mlstm_conv[layer], mlstm_gate_bias[layer],
                        mlstm_out_norm[layer], ret_out_norm[layer], w_branch_att[layer],
                        w_branch_mlstm[layer], w_branch_ret[layer], w_out[layer], w_ffn_in[layer],
                        w_ffn_out[layer])
        outs.append(xb)
    return jnp.stack(outs, axis=0)
```

```python
import functools
import math

import jax
import jax.numpy as jnp
import numpy as np
from jax import lax
from jax.experimental import pallas as pl
from jax.experimental.pallas import tpu as pltpu

F32 = jnp.float32
BF16 = jnp.bfloat16

D_MODEL = 2048
ATT_HEADS = 8
ATT_HEAD_DIM = 128
IDX_HEADS = 8
IDX_DIM = 64
INDEX_TOPK = 256
MLSTM_HEADS = 4
MLSTM_QK_DIM = 128
MLSTM_V_DIM = 256
MLSTM_CHUNK = 256
CONV_WIDTH = 4
RET_HEADS = 8
RET_QK_DIM = 128
RET_V_DIM = 128
RET_CHUNK = 256
ROPE_THETA = 10000.0
NORM_EPS = 1e-6
FFN_DIM = 5632
N_BRANCHES = 3

ATT_W = ATT_HEADS * ATT_HEAD_DIM
MLSTM_QK_W = MLSTM_HEADS * MLSTM_QK_DIM
MLSTM_V_W = MLSTM_HEADS * MLSTM_V_DIM
RET_W = RET_HEADS * RET_V_DIM

_SEGMENTS = (
    ("aq", ATT_W), ("ak", ATT_W), ("av", ATT_W), ("iq", IDX_HEADS * IDX_DIM), ("ik", IDX_DIM),
    ("iw", IDX_HEADS), ("mq", MLSTM_QK_W), ("mk", MLSTM_QK_W), ("mv", MLSTM_V_W),
    ("mi", MLSTM_HEADS), ("mf", MLSTM_HEADS), ("mo", MLSTM_V_W), ("rq", RET_W), ("rk", RET_W),
    ("rv", RET_W), ("rg", RET_W), ("gates", N_BRANCHES * D_MODEL),
)
_SEG_SIZE = dict(_SEGMENTS)
_SEG_START = {}
_off = 0
for _name, _size in _SEGMENTS:
    _SEG_START[_name] = _off
    _off += _size

_MAIN_ORDER = ("aq", "ak", "av", "mv", "mo", "rq", "rk", "rv", "rg", "gates", "iq", "mq", "mk")
_TAIL_ORDER = ("ik", "iw", "mi", "mf")
_MAIN_START = {}
_off = 0
for _name in _MAIN_ORDER:
    _MAIN_START[_name] = _off
    _off += _SEG_SIZE[_name]
MAIN_W = _off
_TAIL_START = {}
_off = 0
for _name in _TAIL_ORDER:
    _TAIL_START[_name] = _off
    _off += _SEG_SIZE[_name]
LANES = 128
TAIL_W = LANES

VMEM_LIMIT = 56 * 1024 * 1024
MASK_VALUE = -1e30


def _cparams(*semantics):
    return pltpu.CompilerParams(dimension_semantics=semantics, vmem_limit_bytes=VMEM_LIMIT)


def _rms_rows(x, g):
    ms = jnp.mean(x * x, axis=-1, keepdims=True)
    return x * lax.rsqrt(ms + NORM_EPS) * g


def _rms_norm_kernel(x_ref, g_ref, o_ref):
    o_ref[...] = _rms_rows(x_ref[...], g_ref[...]).astype(o_ref.dtype)


def rms_norm_bf16(x, g, *, tm):
    m, k = x.shape
    return pl.pallas_call(
        _rms_norm_kernel,
        out_shape=jax.ShapeDtypeStruct((m, k), BF16),
        grid=(m // tm,),
        in_specs=[pl.BlockSpec((tm, k), lambda i: (i, 0)), pl.BlockSpec((1, k), lambda i: (0, 0))],
        out_specs=pl.BlockSpec((tm, k), lambda i: (i, 0)),
        compiler_params=_cparams("parallel"),
        name="rms_norm",
    )(x, g.reshape(1, k))


def _matmul_kernel(a_ref, w_ref, o_ref):
    o_ref[...] = jnp.dot(a_ref[...], w_ref[...].astype(BF16), preferred_element_type=F32)


def matmul(a, w, *, tm, tn):
    m, k = a.shape
    n = w.shape[1]
    return pl.pallas_call(
        _matmul_kernel,
        out_shape=jax.ShapeDtypeStruct((m, n), F32),
        grid=(m // tm, n // tn),
        in_specs=[
            pl.BlockSpec((tm, k), lambda i, j: (i, 0)),
            pl.BlockSpec((k, tn), lambda i, j: (0, j)),
        ],
        out_specs=pl.BlockSpec((tm, tn), lambda i, j: (i, j)),
        compiler_params=_cparams("parallel", "arbitrary"),
        name="matmul",
    )(a, w)


REALIGN_TN = 512


def _realign_kernel(base_ref, shift_ref, *refs):
    o_ref = refs[-1]
    tiles = refs[:-1]
    shift = shift_ref[pl.program_id(0)]
    rot = lax.rem(LANES - shift, LANES)
    keep = lax.broadcasted_iota(jnp.int32, tiles[0].shape, 1) < LANES - shift
    rolled = [pltpu.roll(t[...], rot, 1) for t in tiles]
    for t in range(len(tiles) - 1):
        o_ref[:, t * LANES:(t + 1) * LANES] = jnp.where(keep, rolled[t], rolled[t + 1]).astype(o_ref.dtype)


def realigned_projection_weight(w_in_layer):
    k = w_in_layer.shape[0]
    n_tiles = REALIGN_TN // LANES
    last_tile = (w_in_layer.shape[1] - 1) // LANES
    src = []
    for name in _MAIN_ORDER:
        assert _SEG_SIZE[name] % REALIGN_TN == 0
        src += [_SEG_START[name] + c for c in range(0, _SEG_SIZE[name], REALIGN_TN)]
    base = jnp.asarray([s // LANES for s in src], jnp.int32)
    shift = jnp.asarray([s % LANES for s in src], jnp.int32)

    def tile_spec(t):
        return pl.BlockSpec((k, LANES), lambda j, base, shift: (0, jnp.minimum(base[j] + t, last_tile)))

    grid_spec = pltpu.PrefetchScalarGridSpec(
        num_scalar_prefetch=2,
        grid=(len(src),),
        in_specs=[tile_spec(t) for t in range(n_tiles + 1)],
        out_specs=pl.BlockSpec((k, REALIGN_TN), lambda j, base, shift: (0, j)),
    )
    return pl.pallas_call(
        _realign_kernel,
        out_shape=jax.ShapeDtypeStruct((k, MAIN_W), BF16),
        grid_spec=grid_spec,
        compiler_params=_cparams("arbitrary"),
        name="realign_weight",
    )(base, shift, *([w_in_layer] * (n_tiles + 1)))


def _swiglu_kernel(h_ref, wg_ref, wu_ref, o_ref):
    h = h_ref[...]
    gate = jnp.dot(h, wg_ref[...].astype(BF16), preferred_element_type=F32)
    up = jnp.dot(h, wu_ref[...].astype(BF16), preferred_element_type=F32)
    o_ref[...] = (gate * jax.nn.sigmoid(gate) * up).astype(o_ref.dtype)


def swiglu(h, w_in, *, tm, tn):
    m, k = h.shape
    f = w_in.shape[1] // 2
    nj = f // tn
    return pl.pallas_call(
        _swiglu_kernel,
        out_shape=jax.ShapeDtypeStruct((m, f), BF16),
        grid=(m // tm, nj),
        in_specs=[
            pl.BlockSpec((tm, k), lambda i, j: (i, 0)),
            pl.BlockSpec((k, tn), lambda i, j: (0, j)),
            pl.BlockSpec((k, tn), lambda i, j: (0, j + nj)),
        ],
        out_specs=pl.BlockSpec((tm, tn), lambda i, j: (i, j)),
        compiler_params=_cparams("parallel", "arbitrary"),
        name="swiglu",
    )(h, w_in, w_in)


def _residual_matmul_kernel(r_ref, a_ref, w_ref, o_ref):
    o_ref[...] = r_ref[...] + jnp.dot(a_ref[...], w_ref[...].astype(BF16), preferred_element_type=F32)


def residual_matmul(res, a, w, *, tm, tn):
    m, k = a.shape
    n = w.shape[1]
    return pl.pallas_call(
        _residual_matmul_kernel,
        out_shape=jax.ShapeDtypeStruct((m, n), F32),
        grid=(m // tm, n // tn),
        in_specs=[
            pl.BlockSpec((tm, tn), lambda i, j: (i, j)),
            pl.BlockSpec((tm, k), lambda i, j: (i, 0)),
            pl.BlockSpec((k, tn), lambda i, j: (0, j)),
        ],
        out_specs=pl.BlockSpec((tm, tn), lambda i, j: (i, j)),
        compiler_params=_cparams("parallel", "arbitrary"),
        name="residual_matmul",
    )(res, a, w)


def _merge_kernel(ya_ref, ym_ref, yr_ref, wa_ref, wm_ref, wr_ref, ga_ref, gm_ref, gr_ref, o_ref):
    def branch(y_ref, w_ref, g_ref):
        proj = jnp.dot(y_ref[...], w_ref[...].astype(BF16), preferred_element_type=F32)
        return jax.nn.sigmoid(g_ref[...]) * proj

    merged = branch(ya_ref, wa_ref, ga_ref) + branch(ym_ref, wm_ref, gm_ref) + branch(yr_ref, wr_ref, gr_ref)
    o_ref[...] = merged.astype(o_ref.dtype)


def gated_merge(y_att, y_mlstm, y_ret, w_att, w_mlstm, w_ret, proj, *, tm, tn):
    m, k = y_att.shape
    n = w_att.shape[1]
    gate_blk = _MAIN_START["gates"] // tn
    per_branch = n // tn
    y_spec = pl.BlockSpec((tm, k), lambda i, j: (i, 0))
    w_spec = pl.BlockSpec((k, tn), lambda i, j: (0, j))

    def g_spec(b):
        return pl.BlockSpec((tm, tn), lambda i, j: (i, gate_blk + b * per_branch + j))

    return pl.pallas_call(
        _merge_kernel,
        out_shape=jax.ShapeDtypeStruct((m, n), BF16),
        grid=(m // tm, n // tn),
        in_specs=[y_spec, y_spec, y_spec, w_spec, w_spec, w_spec, g_spec(0), g_spec(1), g_spec(2)],
        out_specs=pl.BlockSpec((tm, tn), lambda i, j: (i, j)),
        compiler_params=_cparams("parallel", "arbitrary"),
        name="gated_merge",
    )(y_att, y_mlstm, y_ret, w_att, w_mlstm, w_ret, proj, proj, proj)


INT_MIN = -(2 ** 31)


def _sortable(x):
    bits = pltpu.bitcast(x, jnp.int32)
    bits = jnp.where(bits == INT_MIN, 0, bits)
    return bits ^ ((bits >> 31) & 0x7FFFFFFF)


GROUP = 256


def _bit_transpose32(words):
    a = list(words)
    j, m = 16, 0x0000FFFF
    while j:
        k = 0
        while k < 32:
            t = (a[k] ^ lax.shift_right_logical(a[k + j], jnp.int32(j))) & m
            a[k] = a[k] ^ t
            a[k + j] = a[k + j] ^ (t << j)
            k = (k + j + 1) & ~j
        j >>= 1
        m = (m ^ (m << j)) & 0xFFFFFFFF
    return a


def _select_kernel(ik_ref, iqt_ref, wt_ref, bias_ref, key_ref, plane_ref, alive_ref, *, tq, tk, topk):
    s_len = ik_ref.shape[0]
    i = pl.program_id(0)
    q_lo = i * tq
    n_groups = (q_lo + tq + GROUP - 1) // GROUP
    n_chunks = n_groups * (GROUP // tk)
    q_pos = q_lo + lax.broadcasted_iota(jnp.int32, (tk, tq), 1)
    row_iota = lax.broadcasted_iota(jnp.int32, (tk, tq), 0)

    @pl.when(i == 0)
    def _():
        plane_ref[...] = jnp.zeros(plane_ref.shape, jnp.int32)

    def score_group(g, carry, *, diagonal):
        g0 = pl.multiple_of(g * GROUP, GROUP)
        slabs = []
        for sub in range(GROUP // tk):
            k0 = g0 + sub * tk
            ik = ik_ref[pl.ds(k0, tk), :]
            acc = jnp.zeros((tk, tq), F32)
            for h in range(IDX_HEADS):
                lg = jnp.dot(ik, iqt_ref[h * IDX_DIM:(h + 1) * IDX_DIM, :], preferred_element_type=F32)
                acc = acc + wt_ref[h:h + 1, :] * jnp.maximum(lg, 0.0)
            key = _sortable(acc)
            if diagonal:
                key = jnp.where(k0 + row_iota <= q_pos, key, INT_MIN)
            key_ref[pl.ds(k0, tk), :] = key
            slabs += [key[8 * j:8 * j + 8, :] for j in range(tk // 8)]
        planes = _bit_transpose32(slabs)
        r0 = pl.multiple_of(g * 8, 8)
        for p in range(32):
            word = planes[31 - p]
            plane_ref[p, pl.ds(r0, 8), :] = ~word if p == 31 else word
        return carry

    assert tq <= GROUP
    lax.fori_loop(0, n_groups - 1, functools.partial(score_group, diagonal=False), 0)
    score_group(n_groups - 1, 0, diagonal=True)

    grp = lax.broadcasted_iota(jnp.int32, alive_ref.shape, 0) >> 3
    alive_ref[...] = jnp.where(grp < n_groups, -1, 0)
    n_gt = jnp.zeros((1, tq), jnp.int32)
    thr_u = jnp.zeros((1, tq), jnp.int32)
    for p in range(31, -1, -1):
        alive = alive_ref[...]
        ones = alive & plane_ref[p]
        c = jnp.sum(lax.population_count(ones), axis=0, keepdims=True)
        take1 = (n_gt + c) >= topk
        alive_ref[...] = jnp.where(take1, ones, alive ^ ones)
        n_gt = n_gt + jnp.where(take1, 0, c)
        thr_u = thr_u | jnp.where(take1, INT_MIN if p == 31 else (1 << p), 0)
    thr = thr_u ^ INT_MIN
    n_eq = jnp.sum(lax.population_count(alive_ref[...]), axis=0, keepdims=True)

    need = topk - n_gt
    excess = jnp.max(n_eq - need)

    def count(pred):
        def body(c, acc):
            k0 = pl.multiple_of(c * tk, tk)
            hit = pred(key_ref[pl.ds(k0, tk), :], k0).astype(jnp.int32)
            return acc + jnp.sum(hit.reshape(tk // 8, 8, tq), axis=0)

        acc = lax.fori_loop(0, n_chunks, body, jnp.zeros((8, tq), jnp.int32))
        return jnp.sum(acc, axis=0, keepdims=True)

    def tie_cut():
        cut = jnp.zeros((1, tq), jnp.int32)
        n_bits = max(1, int(math.ceil(math.log2(s_len))))
        for bit in range(n_bits - 1, -1, -1):
            cand = cut | (1 << bit)
            n_before = count(lambda key, k0, cand=cand: (key == thr) & (k0 + row_iota < cand))
            cut = jnp.where(n_before < need, cand, cut)
        return cut

    cut = lax.cond(excess > 0, tie_cut, lambda: jnp.full((1, tq), s_len, jnp.int32))

    def bias_chunk(c, carry):
        k0 = pl.multiple_of(c * tk, tk)
        key = key_ref[pl.ds(k0, tk), :]
        k_pos = k0 + row_iota
        sel = (key > thr) | ((key == thr) & (k_pos <= cut))
        sel = sel & (k_pos <= q_pos)
        bias_ref[pl.ds(k0, tk), :] = jnp.where(sel, 0.0, MASK_VALUE).astype(bias_ref.dtype)
        return carry

    lax.fori_loop(0, n_chunks, bias_chunk, 0)

    def fill_chunk(c, carry):
        k0 = pl.multiple_of(c * tk, tk)
        bias_ref[pl.ds(k0, tk), :] = jnp.full((tk, tq), MASK_VALUE, bias_ref.dtype)
        return carry

    lax.fori_loop(n_chunks, s_len // tk, fill_chunk, 0)


def select_bias(ik, iqt, wt, *, tq, tk):
    s_len = ik.shape[0]
    topk = min(INDEX_TOPK, s_len // 4)
    kern = functools.partial(_select_kernel, tq=tq, tk=tk, topk=topk)
    return pl.pallas_call(
        kern,
        out_shape=jax.ShapeDtypeStruct((s_len, s_len), BF16),
        grid=(s_len // tq,),
        in_specs=[
            pl.BlockSpec((s_len, IDX_DIM), lambda i: (0, 0)),
            pl.BlockSpec((IDX_HEADS * IDX_DIM, tq), lambda i: (0, i)),
            pl.BlockSpec((IDX_HEADS, tq), lambda i: (0, i)),
        ],
        out_specs=pl.BlockSpec((s_len, tq), lambda i: (0, i)),
        scratch_shapes=[
            pltpu.VMEM((s_len, tq), jnp.int32),
            pltpu.VMEM((32, s_len // 32, tq), jnp.int32),
            pltpu.VMEM((s_len // 32, tq), jnp.int32),
        ],
        compiler_params=_cparams("arbitrary"),
        name="select_bias",
    )(ik, iqt, wt)


ATT_Q_SCALE = ATT_HEAD_DIM ** -0.5 * math.log2(math.e)


VT_ONES_ROWS = 16
VT_ROWS = ATT_HEAD_DIM + VT_ONES_ROWS


def _attention_kernel(qi_ref, ki_ref, qt_ref, k_ref, vt_ref, bias_ref, o_ref, m_ref, acc_ref,
                      s_ref, *, tq, tk):
    step = pl.program_id(0)
    i = qi_ref[step]
    kb = ki_ref[step]
    last_kb = (i * tq + tq - 1) // tk
    dh = ATT_HEAD_DIM

    @pl.when(kb == 0)
    def _():
        m_ref[...] = jnp.full(m_ref.shape, MASK_VALUE, F32)
        acc_ref[...] = jnp.zeros(acc_ref.shape, F32)

    bias = bias_ref[...].astype(F32)

    m_new = []
    for h in range(ATT_HEADS):
        hs = slice(h * dh, (h + 1) * dh)
        s = jnp.dot(k_ref[:, hs], qt_ref[hs, :], preferred_element_type=F32) + bias
        s_ref[h] = s
        m_new.append(jnp.maximum(m_ref[h], jnp.max(s, axis=0, keepdims=True)))
    for h in range(ATT_HEADS):
        alpha = jnp.exp2(m_ref[h] - m_new[h])
        p = jnp.exp2(s_ref[h] - m_new[h])
        acc_ref[h] = alpha * acc_ref[h] + jnp.dot(vt_ref[h * VT_ROWS:(h + 1) * VT_ROWS, :],
                                                  p.astype(BF16), preferred_element_type=F32)
        m_ref[h] = m_new[h]

    @pl.when(kb == last_kb)
    def _():
        for h in range(ATT_HEADS):
            out_t = acc_ref[h, :dh, :] / acc_ref[h, dh:dh + 1, :]
            o_ref[:, h * dh:(h + 1) * dh] = out_t.T.astype(o_ref.dtype)


def masked_attention(qt, k, vt, bias, *, tq, tk):
    width, s_len = qt.shape
    pairs = [(i, kb) for i in range(s_len // tq) for kb in range((i * tq + tq - 1) // tk + 1)]
    qi = jnp.asarray([p[0] for p in pairs], jnp.int32)
    ki = jnp.asarray([p[1] for p in pairs], jnp.int32)
    grid_spec = pltpu.PrefetchScalarGridSpec(
        num_scalar_prefetch=2,
        grid=(len(pairs),),
        in_specs=[
            pl.BlockSpec((width, tq), lambda s, qi, ki: (0, qi[s])),
            pl.BlockSpec((tk, width), lambda s, qi, ki: (ki[s], 0)),
            pl.BlockSpec((ATT_HEADS * VT_ROWS, tk), lambda s, qi, ki: (0, ki[s])),
            pl.BlockSpec((tk, tq), lambda s, qi, ki: (ki[s], qi[s])),
        ],
        out_specs=pl.BlockSpec((tq, width), lambda s, qi, ki: (qi[s], 0)),
        scratch_shapes=[
            pltpu.VMEM((ATT_HEADS, 1, tq), F32),
            pltpu.VMEM((ATT_HEADS, VT_ROWS, tq), F32),
            pltpu.VMEM((ATT_HEADS, tk, tq), F32),
        ],
    )
    return pl.pallas_call(
        functools.partial(_attention_kernel, tq=tq, tk=tk),
        out_shape=jax.ShapeDtypeStruct((s_len, width), BF16),
        grid_spec=grid_spec,
        compiler_params=_cparams("arbitrary"),
        name="masked_attention",
    )(qi, ki, qt, k, vt, bias)


HALO = 8


def _mlstm_kernel(q_ref, k_ref, v_ref, o_ref, gcol_ref, grow_ref, gain_ref, cw_ref, y_ref,
                  c_ref, n_ref, m_ref, halo_ref, qk_ref, qc_ref):
    L = q_ref.shape[0]
    dk, dv, nh = MLSTM_QK_DIM, MLSTM_V_DIM, MLSTM_HEADS
    qk_w = nh * dk

    @pl.when(pl.program_id(0) == 0)
    def _():
        c_ref[...] = jnp.zeros(c_ref.shape, F32)
        n_ref[...] = jnp.zeros(n_ref.shape, F32)
        m_ref[...] = jnp.zeros(m_ref.shape, F32)
        halo_ref[...] = jnp.zeros(halo_ref.shape, F32)

    def causal_conv(x_ref, slot, taps):
        x = x_ref[...]
        ext = jnp.concatenate([halo_ref[slot], x], axis=0)
        out = x * taps[CONV_WIDTH - 1:CONV_WIDTH, :]
        for j in range(CONV_WIDTH - 1):
            lo = HALO - (CONV_WIDTH - 1 - j)
            out = out + ext[lo:lo + L, :] * taps[j:j + 1, :]
        halo_ref[slot] = x[L - HALO:, :]
        return out

    q_all = causal_conv(q_ref, 0, cw_ref[:, :qk_w])
    k_all = causal_conv(k_ref, 1, cw_ref[:, qk_w:])

    tril = lax.broadcasted_iota(jnp.int32, (L, L), 1) <= lax.broadcasted_iota(jnp.int32, (L, L), 0)
    gcol = gcol_ref[...]
    grow = grow_ref[0]
    stats = []
    for h in range(nh):
        q = q_all[:, h * dk:(h + 1) * dk]
        k = k_all[:, h * dk:(h + 1) * dk] * (dk ** -0.5)
        b_col = gcol[:, h:h + 1]
        i_col = gcol[:, nh + h:nh + h + 1]
        b_row = grow[h:h + 1, :]
        i_row = grow[nh + h:nh + h + 1, :]
        m_prev = m_ref[h]
        c_prev = c_ref[h]
        n_prev = n_ref[h]
        qb = q.astype(BF16)
        vb = v_ref[:, h * dv:(h + 1) * dv].astype(BF16)
        qk_ref[h] = lax.dot_general(qb, k.astype(BF16), (((1,), (1,)), ((), ())),
                                    preferred_element_type=F32)
        qc_ref[h] = jnp.dot(qb, c_prev.astype(BF16), preferred_element_type=F32)
        qn = jnp.sum(q * n_prev, axis=-1, keepdims=True)

        b_end = b_row[:, L - 1:L]
        end_row = b_end - b_row + i_row
        m_new = jnp.maximum(b_end + m_prev, jnp.max(end_row, axis=-1, keepdims=True))
        decay = jnp.exp(b_end + m_prev - m_new)
        end_col = b_end - b_col + i_col
        ka = k * jnp.exp(end_col - m_new)
        c_ref[h] = decay * c_prev + jnp.dot(ka.T.astype(BF16), vb, preferred_element_type=F32)
        n_ref[h] = decay * n_prev + jnp.sum(ka, axis=0, keepdims=True)
        m_ref[h] = m_new
        stats.append((b_col, b_row, i_row, m_prev, qn))

    for h in range(nh):
        b_col, b_row, i_row, m_prev, qn = stats[h]
        vb = v_ref[:, h * dv:(h + 1) * dv].astype(BF16)
        log_intra = jnp.where(tril, b_col - b_row + i_row, -jnp.inf)
        m_inter = b_col + m_prev
        m_t = jnp.maximum(m_inter, jnp.max(log_intra, axis=-1, keepdims=True))
        w_inter = jnp.exp(m_inter - m_t)
        p = jnp.exp(log_intra - m_t) * qk_ref[h]
        num = w_inter * qc_ref[h] + jnp.dot(p.astype(BF16), vb, preferred_element_type=F32)
        den = w_inter * qn + jnp.sum(p, axis=-1, keepdims=True)
        h_out = num / jnp.maximum(jnp.abs(den), jnp.exp(-m_t))

        gain = gain_ref[:, h * dv:(h + 1) * dv]
        normed = h_out * lax.rsqrt(jnp.mean(h_out * h_out, axis=-1, keepdims=True) + NORM_EPS) * gain
        y_ref[:, h * dv:(h + 1) * dv] = (normed * jax.nn.sigmoid(o_ref[:, h * dv:(h + 1) * dv])).astype(y_ref.dtype)


def mlstm_mixer(proj, gcol, grow, gain, conv_w):
    s_len = proj.shape[0]
    L = MLSTM_CHUNK
    q_blk = _MAIN_START["mq"] // MLSTM_QK_W
    k_blk = _MAIN_START["mk"] // MLSTM_QK_W
    v_blk = _MAIN_START["mv"] // MLSTM_V_W
    o_blk = _MAIN_START["mo"] // MLSTM_V_W
    return pl.pallas_call(
        _mlstm_kernel,
        out_shape=jax.ShapeDtypeStruct((s_len, MLSTM_V_W), BF16),
        grid=(s_len // L,),
        in_specs=[
            pl.BlockSpec((L, MLSTM_QK_W), lambda c: (c, q_blk)),
            pl.BlockSpec((L, MLSTM_QK_W), lambda c: (c, k_blk)),
            pl.BlockSpec((L, MLSTM_V_W), lambda c: (c, v_blk)),
            pl.BlockSpec((L, MLSTM_V_W), lambda c: (c, o_blk)),
            pl.BlockSpec((L, 2 * MLSTM_HEADS), lambda c: (c, 0)),
            pl.BlockSpec((1, 2 * MLSTM_HEADS, L), lambda c: (c, 0, 0)),
            pl.BlockSpec((1, MLSTM_V_W), lambda c: (0, 0)),
            pl.BlockSpec((CONV_WIDTH, 2 * MLSTM_QK_W), lambda c: (0, 0)),
        ],
        out_specs=pl.BlockSpec((L, MLSTM_V_W), lambda c: (c, 0)),
        scratch_shapes=[
            pltpu.VMEM((MLSTM_HEADS, MLSTM_QK_DIM, MLSTM_V_DIM), F32),
            pltpu.VMEM((MLSTM_HEADS, 1, MLSTM_QK_DIM), F32),
            pltpu.VMEM((MLSTM_HEADS, 1, 1), F32),
            pltpu.VMEM((2, HALO, MLSTM_QK_W), F32),
            pltpu.VMEM((MLSTM_HEADS, L, L), F32),
            pltpu.VMEM((MLSTM_HEADS, L, MLSTM_V_DIM), F32),
        ],
        compiler_params=_cparams("arbitrary"),
        name="mlstm_mixer",
    )(proj, proj, proj, proj, gcol, grow, gain.reshape(1, MLSTM_V_W), conv_w)


def _rope_full(x, cos2, sin2):
    return x * cos2 + pltpu.roll(x, x.shape[-1] // 2, 1) * sin2


def _retention_kernel(q_ref, k_ref, v_ref, g_ref, cos_ref, sin_ref, dint_ref, qdec_ref, kdec_ref,
                      cdec_ref, gain_ref, y_ref, r_ref, qk_ref, cross_ref):
    dk, dv = RET_QK_DIM, RET_V_DIM

    @pl.when(pl.program_id(0) == 0)
    def _():
        r_ref[...] = jnp.zeros(r_ref.shape, F32)

    cos2 = cos_ref[...]
    sin2 = sin_ref[...]
    for h in range(RET_HEADS):
        q = _rope_full(q_ref[:, h * dk:(h + 1) * dk], cos2, sin2)
        k = _rope_full(k_ref[:, h * dk:(h + 1) * dk], cos2, sin2) * (dk ** -0.5)
        vb = v_ref[:, h * dv:(h + 1) * dv].astype(BF16)
        r_prev = r_ref[h]
        qk_ref[h] = lax.dot_general(q.astype(BF16), k.astype(BF16), (((1,), (1,)), ((), ())),
                                    preferred_element_type=F32)
        cross_ref[h] = jnp.dot((q * qdec_ref[h]).astype(BF16), r_prev.astype(BF16),
                               preferred_element_type=F32)
        kd = k * kdec_ref[h]
        r_ref[h] = cdec_ref[h] * r_prev + jnp.dot(kd.T.astype(BF16), vb, preferred_element_type=F32)

    for h in range(RET_HEADS):
        vb = v_ref[:, h * dv:(h + 1) * dv].astype(BF16)
        inner = jnp.dot((qk_ref[h] * dint_ref[h]).astype(BF16), vb, preferred_element_type=F32)
        out = inner + cross_ref[h]

        mu = jnp.mean(out, axis=-1, keepdims=True)
        xc = out - mu
        normed = xc * lax.rsqrt(jnp.mean(xc * xc, axis=-1, keepdims=True) + NORM_EPS)
        gate = g_ref[:, h * dv:(h + 1) * dv]
        y = normed * gain_ref[:, h * dv:(h + 1) * dv] * (gate * jax.nn.sigmoid(gate))
        y_ref[:, h * dv:(h + 1) * dv] = y.astype(y_ref.dtype)


def _retention_tables(size):
    log_gamma = jnp.log(1.0 - 2.0 ** (-5.0 - jnp.arange(RET_HEADS, dtype=F32)))
    pos = jnp.arange(size, dtype=F32)
    rel = pos[:, None] - pos[None, :]
    decay_intra = jnp.where(rel >= 0, jnp.exp(log_gamma[:, None, None] * jnp.maximum(rel, 0.0)), 0.0)
    q_decay = jnp.exp(log_gamma[:, None] * (pos + 1.0))[:, :, None]
    k_decay = jnp.exp(log_gamma[:, None] * (size - 1.0 - pos))[:, :, None]
    chunk_decay = jnp.exp(log_gamma * size)[:, None, None]
    return decay_intra, q_decay, k_decay, chunk_decay


def retention_mixer(proj, cos2, sin2, gain):
    s_len = proj.shape[0]
    L = RET_CHUNK
    q_blk = _MAIN_START["rq"] // RET_W
    k_blk = _MAIN_START["rk"] // RET_W
    v_blk = _MAIN_START["rv"] // RET_W
    g_blk = _MAIN_START["rg"] // RET_W
    decay_intra, q_decay, k_decay, chunk_decay = _retention_tables(L)
    nh = RET_HEADS
    return pl.pallas_call(
        _retention_kernel,
        out_shape=jax.ShapeDtypeStruct((s_len, RET_W), BF16),
        grid=(s_len // L,),
        in_specs=[
            pl.BlockSpec((L, RET_W), lambda c: (c, q_blk)),
            pl.BlockSpec((L, RET_W), lambda c: (c, k_blk)),
            pl.BlockSpec((L, RET_W), lambda c: (c, v_blk)),
            pl.BlockSpec((L, RET_W), lambda c: (c, g_blk)),
            pl.BlockSpec((L, RET_QK_DIM), lambda c: (c, 0)),
            pl.BlockSpec((L, RET_QK_DIM), lambda c: (c, 0)),
            pl.BlockSpec((nh, L, L), lambda c: (0, 0, 0)),
            pl.BlockSpec((nh, L, 1), lambda c: (0, 0, 0)),
            pl.BlockSpec((nh, L, 1), lambda c: (0, 0, 0)),
            pl.BlockSpec((nh, 1, 1), lambda c: (0, 0, 0)),
            pl.BlockSpec((1, RET_W), lambda c: (0, 0)),
        ],
        out_specs=pl.BlockSpec((L, RET_W), lambda c: (c, 0)),
        scratch_shapes=[
            pltpu.VMEM((nh, RET_QK_DIM, RET_V_DIM), F32),
            pltpu.VMEM((nh, L, L), F32),
            pltpu.VMEM((nh, L, RET_V_DIM), F32),
        ],
        compiler_params=_cparams("arbitrary"),
        name="retention_mixer",
    )(proj, proj, proj, proj, cos2, sin2, decay_intra, q_decay, k_decay, chunk_decay,
      gain.reshape(1, RET_W))


def _rope_tables(s_len, d):
    half = d // 2
    inv_freq = ROPE_THETA ** (-jnp.arange(half, dtype=F32) * 2.0 / d)
    ang = jnp.arange(s_len, dtype=jnp.int32).astype(F32)[:, None] * inv_freq[None, :]
    cos, sin = jnp.cos(ang), jnp.sin(ang)
    reps = LANES // d
    return jnp.tile(jnp.concatenate([cos, cos], -1), (1, reps)), jnp.tile(jnp.concatenate([-sin, sin], -1), (1, reps))


def _rope_pair(x, cos4, sin4):
    half = IDX_DIM // 2
    lane = lax.broadcasted_iota(jnp.int32, x.shape, 1)
    partner = jnp.where((lane % IDX_DIM) < half, pltpu.roll(x, LANES - half, 1), pltpu.roll(x, half, 1))
    return x * cos4 + partner * sin4


def _att_prep_kernel(aq_ref, ak_ref, av_ref, iq_ref, tail_ref, cosa_ref, sina_ref, cosi_ref, sini_ref,
                     gq_ref, gk_ref, qt_ref, k_ref, vt_ref, iqt_ref, ik_ref, wt_ref):
    dh = ATT_HEAD_DIM
    cos_a, sin_a = cosa_ref[...], sina_ref[...]
    cos_i, sin_i = cosi_ref[...], sini_ref[...]
    for h in range(ATT_HEADS):
        hs = slice(h * dh, (h + 1) * dh)
        q = _rope_full(_rms_rows(aq_ref[:, hs], gq_ref[...]), cos_a, sin_a)
        qt_ref[hs, :] = (q * ATT_Q_SCALE).T.astype(qt_ref.dtype)
        k = _rope_full(_rms_rows(ak_ref[:, hs], gk_ref[...]), cos_a, sin_a)
        k_ref[:, hs] = k.astype(k_ref.dtype)
        vt_ref[h * VT_ROWS:h * VT_ROWS + dh, :] = av_ref[:, hs].T.astype(vt_ref.dtype)
        vt_ref[h * VT_ROWS + dh:(h + 1) * VT_ROWS, :] = jnp.ones((VT_ONES_ROWS, vt_ref.shape[1]), vt_ref.dtype)
    for c in range(IDX_HEADS * IDX_DIM // LANES):
        cs = slice(c * LANES, (c + 1) * LANES)
        iqt_ref[cs, :] = _rope_pair(iq_ref[:, cs], cos_i, sin_i).T.astype(iqt_ref.dtype)
    tail = tail_ref[...]
    ik_lo = _TAIL_START["ik"]
    ik_ref[...] = _rope_pair(tail, cos_i, sin_i)[:, ik_lo:ik_lo + IDX_DIM].astype(ik_ref.dtype)
    iw_lo = _TAIL_START["iw"]
    wt_ref[...] = (tail * (IDX_HEADS ** -0.5 * IDX_DIM ** -0.5)).T[iw_lo:iw_lo + IDX_HEADS, :]


def attention_prep(proj, tail, rope_a, rope_i, gq, gk, *, tm):
    s_len = proj.shape[0]
    iq_w = IDX_HEADS * IDX_DIM
    row = lambda width, blk: pl.BlockSpec((tm, width), lambda i: (i, blk))
    col = lambda height: pl.BlockSpec((height, tm), lambda i: (0, i))
    gain = pl.BlockSpec((1, ATT_HEAD_DIM), lambda i: (0, 0))
    return pl.pallas_call(
        _att_prep_kernel,
        out_shape=[
            jax.ShapeDtypeStruct((ATT_W, s_len), BF16),
            jax.ShapeDtypeStruct((s_len, ATT_W), BF16),
            jax.ShapeDtypeStruct((ATT_HEADS * VT_ROWS, s_len), BF16),
            jax.ShapeDtypeStruct((iq_w, s_len), BF16),
            jax.ShapeDtypeStruct((s_len, IDX_DIM), BF16),
            jax.ShapeDtypeStruct((IDX_HEADS, s_len), F32),
        ],
        grid=(s_len // tm,),
        in_specs=[
            row(ATT_W, _MAIN_START["aq"] // ATT_W), row(ATT_W, _MAIN_START["ak"] // ATT_W),
            row(ATT_W, _MAIN_START["av"] // ATT_W), row(iq_w, _MAIN_START["iq"] // iq_w),
            row(TAIL_W, 0), row(LANES, 0), row(LANES, 0), row(LANES, 0), row(LANES, 0), gain, gain,
        ],
        out_specs=[col(ATT_W), row(ATT_W, 0), col(ATT_HEADS * VT_ROWS), col(iq_w), row(IDX_DIM, 0), col(IDX_HEADS)],
        compiler_params=_cparams("parallel"),
        name="attention_prep",
    )(proj, proj, proj, proj, tail, rope_a[0], rope_a[1], rope_i[0], rope_i[1],
      gq.reshape(1, ATT_HEAD_DIM), gk.reshape(1, ATT_HEAD_DIM))


def _tail_weight(w_in_layer):
    cols = lambda name: w_in_layer[:, _SEG_START[name]:_SEG_START[name] + _SEG_SIZE[name]]
    tail_used = sum(_SEG_SIZE[n] for n in _TAIL_ORDER)
    return jnp.concatenate([cols(n) for n in _TAIL_ORDER]
                           + [jnp.zeros((w_in_layer.shape[0], TAIL_W - tail_used), w_in_layer.dtype)], axis=1)


def _tile(n, pref):
    t = min(n, pref)
    assert n % t == 0, (n, pref)
    return t


def _layer(x, rope_a, rope_i, norm_mix, norm_ffn, w_in, att_q_norm, att_k_norm, mlstm_conv,
           mlstm_gate_bias, mlstm_out_norm, ret_out_norm, w_branch_att, w_branch_mlstm, w_branch_ret,
           w_out, w_ffn_in, w_ffn_out):
    s_len = x.shape[0]
    tm = _tile(s_len, 1024)
    w_main = realigned_projection_weight(w_in)
    w_tail = _tail_weight(w_in)
    h = rms_norm_bf16(x, norm_mix, tm=_tile(s_len, 512))
    proj = matmul(h, w_main, tm=_tile(s_len, 2048), tn=512)
    tail = matmul(h, w_tail, tm=_tile(s_len, 2048), tn=TAIL_W)

    def tseg(name):
        return tail[:, _TAIL_START[name]:_TAIL_START[name] + _SEG_SIZE[name]]

    tq = _tile(s_len, 256)
    qt, kk, vt, iqt, ik, wt = attention_prep(proj, tail, rope_a, rope_i, att_q_norm, att_k_norm, tm=tq)
    bias = select_bias(ik, iqt, wt, tq=tq, tk=_tile(s_len, 128))
    y_att = masked_attention(qt, kk, vt, bias, tq=tq, tk=_tile(s_len, 512))

    i_pre = tseg("mi") + mlstm_gate_bias[:MLSTM_HEADS]
    f_pre = tseg("mf") + mlstm_gate_bias[MLSTM_HEADS:]
    n_chunks = s_len // MLSTM_CHUNK
    logf = jax.nn.log_sigmoid(f_pre).reshape(n_chunks, MLSTM_CHUNK, MLSTM_HEADS)
    b_cum = jnp.cumsum(logf, axis=1).reshape(s_len, MLSTM_HEADS)
    gcol = jnp.concatenate([b_cum, i_pre], axis=-1)
    grow = gcol.reshape(n_chunks, MLSTM_CHUNK, 2 * MLSTM_HEADS).transpose(0, 2, 1)
    y_mlstm = mlstm_mixer(proj, gcol, grow, mlstm_out_norm.reshape(-1), mlstm_conv)

    y_ret = retention_mixer(proj, rope_a[0], rope_a[1], ret_out_norm.reshape(-1))

    merged = gated_merge(y_att, y_mlstm, y_ret, w_branch_att, w_branch_mlstm, w_branch_ret, proj,
                         tm=tm, tn=512)
    x = residual_matmul(x, merged, w_out, tm=_tile(s_len, 2048), tn=512)

    h_ffn = rms_norm_bf16(x, norm_ffn, tm=_tile(s_len, 512))
    act = swiglu(h_ffn, w_ffn_in, tm=_tile(s_len, 2048), tn=512)
    x = residual_matmul(x, act, w_ffn_out, tm=tm, tn=256)
    return x


def kernel(x, norm_mix, norm_ffn, w_in, att_q_norm, att_k_norm, mlstm_conv, mlstm_gate_bias,
           mlstm_out_norm, ret_out_norm, w_branch_att, w_branch_mlstm, w_branch_ret, w_out,
           w_ffn_in, w_ffn_out):
    b, s_len, d = x.shape
    assert d == D_MODEL and s_len % 256 == 0
    rope_a = _rope_tables(s_len, ATT_HEAD_DIM)
    rope_i = _rope_tables(s_len, IDX_DIM)
    outs = []
    for bi in range(b):
        xb = x[bi]
        for layer in range(w_in.shape[0]):
            xb = _layer(xb, rope_a, rope_i, norm_mix[layer], norm_ffn[layer], w_in[layer], att_q_norm[layer],
                        att_k_norm[layer], mlstm_conv[layer], mlstm_gate_bias[layer],
                        mlstm_out_norm[layer], ret_out_norm[layer], w_branch_att[layer],
                        w_branch_mlstm[layer], w_branch_ret[layer], w_out[layer], w_ffn_in[layer],
                        w_ffn_out[layer])
        outs.append(xb)
    return jnp.stack(outs, axis=0)
```

```python
import functools
import math

import jax
import jax.numpy as jnp
import numpy as np
from jax import lax
from jax.experimental import pallas as pl
from jax.experimental.pallas import tpu as pltpu

F32 = jnp.float32
BF16 = jnp.bfloat16

D_MODEL = 2048
ATT_HEADS = 8
ATT_HEAD_DIM = 128
IDX_HEADS = 8
IDX_DIM = 64
INDEX_TOPK = 256
MLSTM_HEADS = 4
MLSTM_QK_DIM = 128
MLSTM_V_DIM = 256
MLSTM_CHUNK = 256
CONV_WIDTH = 4
RET_HEADS = 8
RET_QK_DIM = 128
RET_V_DIM = 128
RET_CHUNK = 256
ROPE_THETA = 10000.0
NORM_EPS = 1e-6
FFN_DIM = 5632
N_BRANCHES = 3

ATT_W = ATT_HEADS * ATT_HEAD_DIM
MLSTM_QK_W = MLSTM_HEADS * MLSTM_QK_DIM
MLSTM_V_W = MLSTM_HEADS * MLSTM_V_DIM
RET_W = RET_HEADS * RET_V_DIM

_SEGMENTS = (
    ("aq", ATT_W), ("ak", ATT_W), ("av", ATT_W), ("iq", IDX_HEADS * IDX_DIM), ("ik", IDX_DIM),
    ("iw", IDX_HEADS), ("mq", MLSTM_QK_W), ("mk", MLSTM_QK_W), ("mv", MLSTM_V_W),
    ("mi", MLSTM_HEADS), ("mf", MLSTM_HEADS), ("mo", MLSTM_V_W), ("rq", RET_W), ("rk", RET_W),
    ("rv", RET_W), ("rg", RET_W), ("gates", N_BRANCHES * D_MODEL),
)
_SEG_SIZE = dict(_SEGMENTS)
_SEG_START = {}
_off = 0
for _name, _size in _SEGMENTS:
    _SEG_START[_name] = _off
    _off += _size

_MAIN_ORDER = ("aq", "ak", "av", "mv", "mo", "rq", "rk", "rv", "rg", "gates", "iq", "mq", "mk")
_TAIL_ORDER = ("ik", "iw", "mi", "mf")
_MAIN_START = {}
_off = 0
for _name in _MAIN_ORDER:
    _MAIN_START[_name] = _off
    _off += _SEG_SIZE[_name]
MAIN_W = _off
_TAIL_START = {}
_off = 0
for _name in _TAIL_ORDER:
    _TAIL_START[_name] = _off
    _off += _SEG_SIZE[_name]
LANES = 128
TAIL_W = LANES

VMEM_LIMIT = 56 * 1024 * 1024
MASK_VALUE = -1e30


def _cparams(*semantics):
    return pltpu.CompilerParams(dimension_semantics=semantics, vmem_limit_bytes=VMEM_LIMIT)


def _rms_rows(x, g):
    ms = jnp.mean(x * x, axis=-1, keepdims=True)
    return x * lax.rsqrt(ms + NORM_EPS) * g


def _rms_norm_kernel(x_ref, g_ref, o_ref):
    o_ref[...] = _rms_rows(x_ref[...], g_ref[...]).astype(o_ref.dtype)


def rms_norm_bf16(x, g, *, tm):
    m, k = x.shape
    return pl.pallas_call(
        _rms_norm_kernel,
        out_shape=jax.ShapeDtypeStruct((m, k), BF16),
        grid=(m // tm,),
        in_specs=[pl.BlockSpec((tm, k), lambda i: (i, 0)), pl.BlockSpec((1, k), lambda i: (0, 0))],
        out_specs=pl.BlockSpec((tm, k), lambda i: (i, 0)),
        compiler_params=_cparams("parallel"),
        name="rms_norm",
    )(x, g.reshape(1, k))


PROJ_TN = 512


def _matmul_nt_kernel(*refs):
    a_ref, wt_ref, o_ref = refs[-3:]
    wt = wt_ref[...].reshape(wt_ref.shape[-2:])
    o_ref[...] = lax.dot_general(a_ref[...], wt.astype(BF16), (((1,), (1,)), ((), ())),
                                 preferred_element_type=F32)


def projection_main(h, w_in_t, layer, *, tm):
    m, k = h.shape
    rows = []
    for name in _MAIN_ORDER:
        assert _SEG_SIZE[name] % PROJ_TN == 0 and _SEG_START[name] % 8 == 0
        rows += [_SEG_START[name] + c for c in range(0, _SEG_SIZE[name], PROJ_TN)]
    src = jnp.asarray([r // 8 for r in rows], jnp.int32)
    grid_spec = pltpu.PrefetchScalarGridSpec(
        num_scalar_prefetch=1,
        grid=(m // tm, len(rows)),
        in_specs=[
            pl.BlockSpec((tm, k), lambda i, j, src: (i, 0)),
            pl.BlockSpec((pl.Element(1), pl.Element(PROJ_TN), pl.Element(k)),
                         lambda i, j, src: (layer, src[j] * 8, 0)),
        ],
        out_specs=pl.BlockSpec((tm, PROJ_TN), lambda i, j, src: (i, j)),
    )
    return pl.pallas_call(
        _matmul_nt_kernel,
        out_shape=jax.ShapeDtypeStruct((m, MAIN_W), F32),
        grid_spec=grid_spec,
        compiler_params=_cparams("parallel", "arbitrary"),
        name="projection_main",
    )(src, h, w_in_t)


def projection_tail(h, w_tail_t, *, tm):
    m, k = h.shape
    return pl.pallas_call(
        _matmul_nt_kernel,
        out_shape=jax.ShapeDtypeStruct((m, TAIL_W), F32),
        grid=(m // tm,),
        in_specs=[pl.BlockSpec((tm, k), lambda i: (i, 0)), pl.BlockSpec((TAIL_W, k), lambda i: (0, 0))],
        out_specs=pl.BlockSpec((tm, TAIL_W), lambda i: (i, 0)),
        compiler_params=_cparams("parallel"),
        name="projection_tail",
    )(h, w_tail_t)


def _swiglu_kernel(h_ref, wg_ref, wu_ref, o_ref):
    h = h_ref[...]
    gate = jnp.dot(h, wg_ref[...].astype(BF16), preferred_element_type=F32)
    up = jnp.dot(h, wu_ref[...].astype(BF16), preferred_element_type=F32)
    o_ref[...] = (gate * jax.nn.sigmoid(gate) * up).astype(o_ref.dtype)


def swiglu(h, w_in_stack, layer, *, tm, tn):
    m, k = h.shape
    f = w_in_stack.shape[2] // 2
    nj = f // tn
    w_in = w_in_stack
    return pl.pallas_call(
        _swiglu_kernel,
        out_shape=jax.ShapeDtypeStruct((m, f), BF16),
        grid=(m // tm, nj),
        in_specs=[
            pl.BlockSpec((tm, k), lambda i, j: (i, 0)),
            _layer_weight_spec(layer, k, tn),
            _layer_weight_spec(layer, k, tn, col_offset_blocks=nj),
        ],
        out_specs=pl.BlockSpec((tm, tn), lambda i, j: (i, j)),
        compiler_params=_cparams("parallel", "arbitrary"),
        name="swiglu",
    )(h, w_in, w_in)


def _residual_matmul_kernel(r_ref, a_ref, w_ref, o_ref):
    o_ref[...] = r_ref[...] + jnp.dot(a_ref[...], w_ref[...].astype(BF16), preferred_element_type=F32)


def _layer_weight_spec(layer, k, tn, col_offset_blocks=0):
    return pl.BlockSpec((None, k, tn), lambda i, j: (layer, 0, j + col_offset_blocks))


def residual_matmul(res, a, w_stack, layer, *, tm, tn):
    m, k = a.shape
    n = w_stack.shape[2]
    return pl.pallas_call(
        _residual_matmul_kernel,
        out_shape=jax.ShapeDtypeStruct((m, n), F32),
        grid=(m // tm, n // tn),
        in_specs=[
            pl.BlockSpec((tm, tn), lambda i, j: (i, j)),
            pl.BlockSpec((tm, k), lambda i, j: (i, 0)),
            _layer_weight_spec(layer, k, tn),
        ],
        out_specs=pl.BlockSpec((tm, tn), lambda i, j: (i, j)),
        compiler_params=_cparams("parallel", "arbitrary"),
        name="residual_matmul",
    )(res, a, w_stack)


def _merge_kernel(ya_ref, ym_ref, yr_ref, wa_ref, wm_ref, wr_ref, ga_ref, gm_ref, gr_ref, o_ref):
    def branch(y_ref, w_ref, g_ref):
        proj = jnp.dot(y_ref[...], w_ref[...].astype(BF16), preferred_element_type=F32)
        return jax.nn.sigmoid(g_ref[...]) * proj

    merged = branch(ya_ref, wa_ref, ga_ref) + branch(ym_ref, wm_ref, gm_ref) + branch(yr_ref, wr_ref, gr_ref)
    o_ref[...] = merged.astype(o_ref.dtype)


def gated_merge(y_att, y_mlstm, y_ret, w_att, w_mlstm, w_ret, layer, proj, *, tm, tn):
    m, k = y_att.shape
    n = w_att.shape[2]
    gate_blk = _MAIN_START["gates"] // tn
    per_branch = n // tn
    y_spec = pl.BlockSpec((tm, k), lambda i, j: (i, 0))
    w_spec = _layer_weight_spec(layer, k, tn)

    def g_spec(b):
        return pl.BlockSpec((tm, tn), lambda i, j: (i, gate_blk + b * per_branch + j))

    return pl.pallas_call(
        _merge_kernel,
        out_shape=jax.ShapeDtypeStruct((m, n), BF16),
        grid=(m // tm, n // tn),
        in_specs=[y_spec, y_spec, y_spec, w_spec, w_spec, w_spec, g_spec(0), g_spec(1), g_spec(2)],
        out_specs=pl.BlockSpec((tm, tn), lambda i, j: (i, j)),
        compiler_params=_cparams("parallel", "arbitrary"),
        name="gated_merge",
    )(y_att, y_mlstm, y_ret, w_att, w_mlstm, w_ret, proj, proj, proj)


INT_MIN = -(2 ** 31)


def _sortable(x):
    bits = pltpu.bitcast(x, jnp.int32)
    bits = jnp.where(bits == INT_MIN, 0, bits)
    return bits ^ ((bits >> 31) & 0x7FFFFFFF)


GROUP = 256


def _bit_transpose32(words):
    a = list(words)
    j, m = 16, 0x0000FFFF
    while j:
        k = 0
        while k < 32:
            t = (a[k] ^ lax.shift_right_logical(a[k + j], jnp.int32(j))) & m
            a[k] = a[k] ^ t
            a[k + j] = a[k + j] ^ (t << j)
            k = (k + j + 1) & ~j
        j >>= 1
        m = (m ^ (m << j)) & 0xFFFFFFFF
    return a


def _select_kernel(ik_ref, iqt_ref, wt_ref, bias_ref, key_ref, plane_ref, alive_ref, *, tq, tk, topk):
    s_len = ik_ref.shape[0]
    i = pl.program_id(0)
    q_lo = i * tq
    n_groups = (q_lo + tq + GROUP - 1) // GROUP
    n_chunks = n_groups * (GROUP // tk)
    q_pos = q_lo + lax.broadcasted_iota(jnp.int32, (tk, tq), 1)
    row_iota = lax.broadcasted_iota(jnp.int32, (tk, tq), 0)

    @pl.when(i == 0)
    def _():
        plane_ref[...] = jnp.zeros(plane_ref.shape, jnp.int32)

    def score_group(g, carry, *, diagonal):
        g0 = pl.multiple_of(g * GROUP, GROUP)
        slabs = []
        for sub in range(GROUP // tk):
            k0 = g0 + sub * tk
            ik = ik_ref[pl.ds(k0, tk), :]
            acc = jnp.zeros((tk, tq), F32)
            for h in range(IDX_HEADS):
                lg = jnp.dot(ik, iqt_ref[h * IDX_DIM:(h + 1) * IDX_DIM, :], preferred_element_type=F32)
                acc = acc + wt_ref[h:h + 1, :] * jnp.maximum(lg, 0.0)
            key = _sortable(acc)
            if diagonal:
                key = jnp.where(k0 + row_iota <= q_pos, key, INT_MIN)
            key_ref[pl.ds(k0, tk), :] = key
            slabs += [key[8 * j:8 * j + 8, :] for j in range(tk // 8)]
        planes = _bit_transpose32(slabs)
        r0 = pl.multiple_of(g * 8, 8)
        for p in range(32):
            word = planes[31 - p]
            plane_ref[p, pl.ds(r0, 8), :] = ~word if p == 31 else word
        return carry

    assert tq <= GROUP
    lax.fori_loop(0, n_groups - 1, functools.partial(score_group, diagonal=False), 0)
    score_group(n_groups - 1, 0, diagonal=True)

    grp = lax.broadcasted_iota(jnp.int32, alive_ref.shape, 0) >> 3
    alive_ref[...] = jnp.where(grp < n_groups, -1, 0)
    n_gt = jnp.zeros((1, tq), jnp.int32)
    thr_u = jnp.zeros((1, tq), jnp.int32)
    for p in range(31, -1, -1):
        alive = alive_ref[...]
        ones = alive & plane_ref[p]
        c = jnp.sum(lax.population_count(ones), axis=0, keepdims=True)
        take1 = (n_gt + c) >= topk
        alive_ref[...] = jnp.where(take1, ones, alive ^ ones)
        n_gt = n_gt + jnp.where(take1, 0, c)
        thr_u = thr_u | jnp.where(take1, INT_MIN if p == 31 else (1 << p), 0)
    thr = thr_u ^ INT_MIN
    n_eq = jnp.sum(lax.population_count(alive_ref[...]), axis=0, keepdims=True)

    need = topk - n_gt
    excess = jnp.max(n_eq - need)

    def count(pred):
        def body(c, acc):
            k0 = pl.multiple_of(c * tk, tk)
            hit = pred(key_ref[pl.ds(k0, tk), :], k0).astype(jnp.int32)
            return acc + jnp.sum(hit.reshape(tk // 8, 8, tq), axis=0)

        acc = lax.fori_loop(0, n_chunks, body, jnp.zeros((8, tq), jnp.int32))
        return jnp.sum(acc, axis=0, keepdims=True)

    def tie_cut():
        cut = jnp.zeros((1, tq), jnp.int32)
        n_bits = max(1, int(math.ceil(math.log2(s_len))))
        for bit in range(n_bits - 1, -1, -1):
            cand = cut | (1 << bit)
            n_before = count(lambda key, k0, cand=cand: (key == thr) & (k0 + row_iota < cand))
            cut = jnp.where(n_before < need, cand, cut)
        return cut

    cut = lax.cond(excess > 0, tie_cut, lambda: jnp.full((1, tq), s_len, jnp.int32))

    def bias_chunk(c, carry):
        k0 = pl.multiple_of(c * tk, tk)
        key = key_ref[pl.ds(k0, tk), :]
        k_pos = k0 + row_iota
        sel = (key > thr) | ((key == thr) & (k_pos <= cut))
        sel = sel & (k_pos <= q_pos)
        bias_ref[pl.ds(k0, tk), :] = jnp.where(sel, 0.0, MASK_VALUE).astype(bias_ref.dtype)
        return carry

    lax.fori_loop(0, n_chunks, bias_chunk, 0)

    def fill_chunk(c, carry):
        k0 = pl.multiple_of(c * tk, tk)
        bias_ref[pl.ds(k0, tk), :] = jnp.full((tk, tq), MASK_VALUE, bias_ref.dtype)
        return carry

    lax.fori_loop(n_chunks, s_len // tk, fill_chunk, 0)


def select_bias(ik, iqt, wt, *, tq, tk):
    s_len = ik.shape[0]
    topk = min(INDEX_TOPK, s_len // 4)
    kern = functools.partial(_select_kernel, tq=tq, tk=tk, topk=topk)
    return pl.pallas_call(
        kern,
        out_shape=jax.ShapeDtypeStruct((s_len, s_len), BF16),
        grid=(s_len // tq,),
        in_specs=[
            pl.BlockSpec((s_len, IDX_DIM), lambda i: (0, 0)),
            pl.BlockSpec((IDX_HEADS * IDX_DIM, tq), lambda i: (0, i)),
            pl.BlockSpec((IDX_HEADS, tq), lambda i: (0, i)),
        ],
        out_specs=pl.BlockSpec((s_len, tq), lambda i: (0, i)),
        scratch_shapes=[
            pltpu.VMEM((s_len, tq), jnp.int32),
            pltpu.VMEM((32, s_len // 32, tq), jnp.int32),
            pltpu.VMEM((s_len // 32, tq), jnp.int32),
        ],
        compiler_params=_cparams("arbitrary"),
        name="select_bias",
    )(ik, iqt, wt)


ATT_Q_SCALE = ATT_HEAD_DIM ** -0.5 * math.log2(math.e)


VT_ONES_ROWS = 16
VT_ROWS = ATT_HEAD_DIM + VT_ONES_ROWS


def _attention_kernel(qi_ref, ki_ref, qt_ref, k_ref, vt_ref, bias_ref, o_ref, m_ref, acc_ref,
                      s_ref, *, tq, tk):
    step = pl.program_id(0)
    i = qi_ref[step]
    kb = ki_ref[step]
    last_kb = (i * tq + tq - 1) // tk
    dh = ATT_HEAD_DIM

    @pl.when(kb == 0)
    def _():
        m_ref[...] = jnp.full(m_ref.shape, MASK_VALUE, F32)
        acc_ref[...] = jnp.zeros(acc_ref.shape, F32)

    bias = bias_ref[...].astype(F32)

    m_new = []
    for h in range(ATT_HEADS):
        hs = slice(h * dh, (h + 1) * dh)
        s = jnp.dot(k_ref[:, hs], qt_ref[hs, :], preferred_element_type=F32) + bias
        s_ref[h] = s
        m_new.append(jnp.maximum(m_ref[h], jnp.max(s, axis=0, keepdims=True)))
    for h in range(ATT_HEADS):
        alpha = jnp.exp2(m_ref[h] - m_new[h])
        p = jnp.exp2(s_ref[h] - m_new[h])
        acc_ref[h] = alpha * acc_ref[h] + jnp.dot(vt_ref[h * VT_ROWS:(h + 1) * VT_ROWS, :],
                                                  p.astype(BF16), preferred_element_type=F32)
        m_ref[h] = m_new[h]

    @pl.when(kb == last_kb)
    def _():
        for h in range(ATT_HEADS):
            out_t = acc_ref[h, :dh, :] / acc_ref[h, dh:dh + 1, :]
            o_ref[:, h * dh:(h + 1) * dh] = out_t.T.astype(o_ref.dtype)


def masked_attention(qt, k, vt, bias, *, tq, tk):
    width, s_len = qt.shape
    pairs = [(i, kb) for i in range(s_len // tq) for kb in range((i * tq + tq - 1) // tk + 1)]
    qi = jnp.asarray([p[0] for p in pairs], jnp.int32)
    ki = jnp.asarray([p[1] for p in pairs], jnp.int32)
    grid_spec = pltpu.PrefetchScalarGridSpec(
        num_scalar_prefetch=2,
        grid=(len(pairs),),
        in_specs=[
            pl.BlockSpec((width, tq), lambda s, qi, ki: (0, qi[s])),
            pl.BlockSpec((tk, width), lambda s, qi, ki: (ki[s], 0)),
            pl.BlockSpec((ATT_HEADS * VT_ROWS, tk), lambda s, qi, ki: (0, ki[s])),
            pl.BlockSpec((tk, tq), lambda s, qi, ki: (ki[s], qi[s])),
        ],
        out_specs=pl.BlockSpec((tq, width), lambda s, qi, ki: (qi[s], 0)),
        scratch_shapes=[
            pltpu.VMEM((ATT_HEADS, 1, tq), F32),
            pltpu.VMEM((ATT_HEADS, VT_ROWS, tq), F32),
            pltpu.VMEM((ATT_HEADS, tk, tq), F32),
        ],
    )
    return pl.pallas_call(
        functools.partial(_attention_kernel, tq=tq, tk=tk),
        out_shape=jax.ShapeDtypeStruct((s_len, width), BF16),
        grid_spec=grid_spec,
        compiler_params=_cparams("arbitrary"),
        name="masked_attention",
    )(qi, ki, qt, k, vt, bias)


HALO = 8


def _mlstm_kernel(q_ref, k_ref, v_ref, o_ref, gcol_ref, grow_ref, gain_ref, cw_ref, y_ref,
                  c_ref, n_ref, m_ref, halo_ref, qk_ref, qc_ref):
    L = q_ref.shape[0]
    dk, dv, nh = MLSTM_QK_DIM, MLSTM_V_DIM, MLSTM_HEADS
    qk_w = nh * dk

    @pl.when(pl.program_id(0) == 0)
    def _():
        c_ref[...] = jnp.zeros(c_ref.shape, F32)
        n_ref[...] = jnp.zeros(n_ref.shape, F32)
        m_ref[...] = jnp.zeros(m_ref.shape, F32)
        halo_ref[...] = jnp.zeros(halo_ref.shape, F32)

    def causal_conv(x_ref, slot, taps):
        x = x_ref[...]
        ext = jnp.concatenate([halo_ref[slot], x], axis=0)
        out = x * taps[CONV_WIDTH - 1:CONV_WIDTH, :]
        for j in range(CONV_WIDTH - 1):
            lo = HALO - (CONV_WIDTH - 1 - j)
            out = out + ext[lo:lo + L, :] * taps[j:j + 1, :]
        halo_ref[slot] = x[L - HALO:, :]
        return out

    q_all = causal_conv(q_ref, 0, cw_ref[:, :qk_w])
    k_all = causal_conv(k_ref, 1, cw_ref[:, qk_w:])

    tril = lax.broadcasted_iota(jnp.int32, (L, L), 1) <= lax.broadcasted_iota(jnp.int32, (L, L), 0)
    gcol = gcol_ref[...]
    grow = grow_ref[0]
    stats = []
    for h in range(nh):
        q = q_all[:, h * dk:(h + 1) * dk]
        k = k_all[:, h * dk:(h + 1) * dk] * (dk ** -0.5)
        b_col = gcol[:, h:h + 1]
        i_col = gcol[:, nh + h:nh + h + 1]
        b_row = grow[h:h + 1, :]
        i_row = grow[nh + h:nh + h + 1, :]
        m_prev = m_ref[h]
        c_prev = c_ref[h]
        n_prev = n_ref[h]
        qb = q.astype(BF16)
        vb = v_ref[:, h * dv:(h + 1) * dv].astype(BF16)
        qk_ref[h] = lax.dot_general(qb, k.astype(BF16), (((1,), (1,)), ((), ())),
                                    preferred_element_type=F32)
        qc_ref[h] = jnp.dot(qb, c_prev.astype(BF16), preferred_element_type=F32)
        qn = jnp.sum(q * n_prev, axis=-1, keepdims=True)

        b_end = b_row[:, L - 1:L]
        end_row = b_end - b_row + i_row
        m_new = jnp.maximum(b_end + m_prev, jnp.max(end_row, axis=-1, keepdims=True))
        decay = jnp.exp(b_end + m_prev - m_new)
        end_col = b_end - b_col + i_col
        ka = k * jnp.exp(end_col - m_new)
        c_ref[h] = decay * c_prev + jnp.dot(ka.T.astype(BF16), vb, preferred_element_type=F32)
        n_ref[h] = decay * n_prev + jnp.sum(ka, axis=0, keepdims=True)
        m_ref[h] = m_new
        stats.append((b_col, b_row, i_row, m_prev, qn))

    for h in range(nh):
        b_col, b_row, i_row, m_prev, qn = stats[h]
        vb = v_ref[:, h * dv:(h + 1) * dv].astype(BF16)
        log_intra = jnp.where(tril, b_col - b_row + i_row, -jnp.inf)
        m_inter = b_col + m_prev
        m_t = jnp.maximum(m_inter, jnp.max(log_intra, axis=-1, keepdims=True))
        w_inter = jnp.exp(m_inter - m_t)
        p = jnp.exp(log_intra - m_t) * qk_ref[h]
        num = w_inter * qc_ref[h] + jnp.dot(p.astype(BF16), vb, preferred_element_type=F32)
        den = w_inter * qn + jnp.sum(p, axis=-1, keepdims=True)
        h_out = num / jnp.maximum(jnp.abs(den), jnp.exp(-m_t))

        gain = gain_ref[:, h * dv:(h + 1) * dv]
        normed = h_out * lax.rsqrt(jnp.mean(h_out * h_out, axis=-1, keepdims=True) + NORM_EPS) * gain
        y_ref[:, h * dv:(h + 1) * dv] = (normed * jax.nn.sigmoid(o_ref[:, h * dv:(h + 1) * dv])).astype(y_ref.dtype)


def mlstm_mixer(proj, gcol, grow, gain, conv_w):
    s_len = proj.shape[0]
    L = MLSTM_CHUNK
    q_blk = _MAIN_START["mq"] // MLSTM_QK_W
    k_blk = _MAIN_START["mk"] // MLSTM_QK_W
    v_blk = _MAIN_START["mv"] // MLSTM_V_W
    o_blk = _MAIN_START["mo"] // MLSTM_V_W
    return pl.pallas_call(
        _mlstm_kernel,
        out_shape=jax.ShapeDtypeStruct((s_len, MLSTM_V_W), BF16),
        grid=(s_len // L,),
        in_specs=[
            pl.BlockSpec((L, MLSTM_QK_W), lambda c: (c, q_blk)),
            pl.BlockSpec((L, MLSTM_QK_W), lambda c: (c, k_blk)),
            pl.BlockSpec((L, MLSTM_V_W), lambda c: (c, v_blk)),
            pl.BlockSpec((L, MLSTM_V_W), lambda c: (c, o_blk)),
            pl.BlockSpec((L, 2 * MLSTM_HEADS), lambda c: (c, 0)),
            pl.BlockSpec((1, 2 * MLSTM_HEADS, L), lambda c: (c, 0, 0)),
            pl.BlockSpec((1, MLSTM_V_W), lambda c: (0, 0)),
            pl.BlockSpec((CONV_WIDTH, 2 * MLSTM_QK_W), lambda c: (0, 0)),
        ],
        out_specs=pl.BlockSpec((L, MLSTM_V_W), lambda c: (c, 0)),
        scratch_shapes=[
            pltpu.VMEM((MLSTM_HEADS, MLSTM_QK_DIM, MLSTM_V_DIM), F32),
            pltpu.VMEM((MLSTM_HEADS, 1, MLSTM_QK_DIM), F32),
            pltpu.VMEM((MLSTM_HEADS, 1, 1), F32),
            pltpu.VMEM((2, HALO, MLSTM_QK_W), F32),
            pltpu.VMEM((MLSTM_HEADS, L, L), F32),
            pltpu.VMEM((MLSTM_HEADS, L, MLSTM_V_DIM), F32),
        ],
        compiler_params=_cparams("arbitrary"),
        name="mlstm_mixer",
    )(proj, proj, proj, proj, gcol, grow, gain.reshape(1, MLSTM_V_W), conv_w)


def _rope_full(x, cos2, sin2):
    return x * cos2 + pltpu.roll(x, x.shape[-1] // 2, 1) * sin2


def _retention_kernel(q_ref, k_ref, v_ref, g_ref, cos_ref, sin_ref, dint_ref, qdec_ref, kdec_ref,
                      cdec_ref, gain_ref, y_ref, r_ref, qk_ref, cross_ref):
    dk, dv = RET_QK_DIM, RET_V_DIM

    @pl.when(pl.program_id(0) == 0)
    def _():
        r_ref[...] = jnp.zeros(r_ref.shape, F32)

    cos2 = cos_ref[...]
    sin2 = sin_ref[...]
    for h in range(RET_HEADS):
        q = _rope_full(q_ref[:, h * dk:(h + 1) * dk], cos2, sin2)
        k = _rope_full(k_ref[:, h * dk:(h + 1) * dk], cos2, sin2) * (dk ** -0.5)
        vb = v_ref[:, h * dv:(h + 1) * dv].astype(BF16)
        r_prev = r_ref[h]
        qk_ref[h] = lax.dot_general(q.astype(BF16), k.astype(BF16), (((1,), (1,)), ((), ())),
                                    preferred_element_type=F32)
        cross_ref[h] = jnp.dot((q * qdec_ref[h]).astype(BF16), r_prev.astype(BF16),
                               preferred_element_type=F32)
        kd = k * kdec_ref[h]
        r_ref[h] = cdec_ref[h] * r_prev + jnp.dot(kd.T.astype(BF16), vb, preferred_element_type=F32)

    for h in range(RET_HEADS):
        vb = v_ref[:, h * dv:(h + 1) * dv].astype(BF16)
        inner = jnp.dot((qk_ref[h] * dint_ref[h]).astype(BF16), vb, preferred_element_type=F32)
        out = inner + cross_ref[h]

        mu = jnp.mean(out, axis=-1, keepdims=True)
        xc = out - mu
        normed = xc * lax.rsqrt(jnp.mean(xc * xc, axis=-1, keepdims=True) + NORM_EPS)
        gate = g_ref[:, h * dv:(h + 1) * dv]
        y = normed * gain_ref[:, h * dv:(h + 1) * dv] * (gate * jax.nn.sigmoid(gate))
        y_ref[:, h * dv:(h + 1) * dv] = y.astype(y_ref.dtype)


def _retention_tables(size):
    log_gamma = jnp.log(1.0 - 2.0 ** (-5.0 - jnp.arange(RET_HEADS, dtype=F32)))
    pos = jnp.arange(size, dtype=F32)
    rel = pos[:, None] - pos[None, :]
    decay_intra = jnp.where(rel >= 0, jnp.exp(log_gamma[:, None, None] * jnp.maximum(rel, 0.0)), 0.0)
    q_decay = jnp.exp(log_gamma[:, None] * (pos + 1.0))[:, :, None]
    k_decay = jnp.exp(log_gamma[:, None] * (size - 1.0 - pos))[:, :, None]
    chunk_decay = jnp.exp(log_gamma * size)[:, None, None]
    return decay_intra, q_decay, k_decay, chunk_decay


def retention_mixer(proj, cos2, sin2, gain):
    s_len = proj.shape[0]
    L = RET_CHUNK
    q_blk = _MAIN_START["rq"] // RET_W
    k_blk = _MAIN_START["rk"] // RET_W
    v_blk = _MAIN_START["rv"] // RET_W
    g_blk = _MAIN_START["rg"] // RET_W
    decay_intra, q_decay, k_decay, chunk_decay = _retention_tables(L)
    nh = RET_HEADS
    return pl.pallas_call(
        _retention_kernel,
        out_shape=jax.ShapeDtypeStruct((s_len, RET_W), BF16),
        grid=(s_len // L,),
        in_specs=[
            pl.BlockSpec((L, RET_W), lambda c: (c, q_blk)),
            pl.BlockSpec((L, RET_W), lambda c: (c, k_blk)),
            pl.BlockSpec((L, RET_W), lambda c: (c, v_blk)),
            pl.BlockSpec((L, RET_W), lambda c: (c, g_blk)),
            pl.BlockSpec((L, RET_QK_DIM), lambda c: (c, 0)),
            pl.BlockSpec((L, RET_QK_DIM), lambda c: (c, 0)),
            pl.BlockSpec((nh, L, L), lambda c: (0, 0, 0)),
            pl.BlockSpec((nh, L, 1), lambda c: (0, 0, 0)),
            pl.BlockSpec((nh, L, 1), lambda c: (0, 0, 0)),
            pl.BlockSpec((nh, 1, 1), lambda c: (0, 0, 0)),
            pl.BlockSpec((1, RET_W), lambda c: (0, 0)),
        ],
        out_specs=pl.BlockSpec((L, RET_W), lambda c: (c, 0)),
        scratch_shapes=[
            pltpu.VMEM((nh, RET_QK_DIM, RET_V_DIM), F32),
            pltpu.VMEM((nh, L, L), F32),
            pltpu.VMEM((nh, L, RET_V_DIM), F32),
        ],
        compiler_params=_cparams("arbitrary"),
        name="retention_mixer",
    )(proj, proj, proj, proj, cos2, sin2, decay_intra, q_decay, k_decay, chunk_decay,
      gain.reshape(1, RET_W))


def _rope_tables(s_len, d):
    half = d // 2
    inv_freq = ROPE_THETA ** (-jnp.arange(half, dtype=F32) * 2.0 / d)
    ang = jnp.arange(s_len, dtype=jnp.int32).astype(F32)[:, None] * inv_freq[None, :]
    cos, sin = jnp.cos(ang), jnp.sin(ang)
    reps = LANES // d
    return jnp.tile(jnp.concatenate([cos, cos], -1), (1, reps)), jnp.tile(jnp.concatenate([-sin, sin], -1), (1, reps))


def _rope_pair(x, cos4, sin4):
    half = IDX_DIM // 2
    lane = lax.broadcasted_iota(jnp.int32, x.shape, 1)
    partner = jnp.where((lane % IDX_DIM) < half, pltpu.roll(x, LANES - half, 1), pltpu.roll(x, half, 1))
    return x * cos4 + partner * sin4


def _att_prep_kernel(aq_ref, ak_ref, av_ref, iq_ref, tail_ref, cosa_ref, sina_ref, cosi_ref, sini_ref,
                     gq_ref, gk_ref, qt_ref, k_ref, vt_ref, iqt_ref, ik_ref, wt_ref):
    dh = ATT_HEAD_DIM
    cos_a, sin_a = cosa_ref[...], sina_ref[...]
    cos_i, sin_i = cosi_ref[...], sini_ref[...]
    for h in range(ATT_HEADS):
        hs = slice(h * dh, (h + 1) * dh)
        q = _rope_full(_rms_rows(aq_ref[:, hs], gq_ref[...]), cos_a, sin_a)
        qt_ref[hs, :] = (q * ATT_Q_SCALE).T.astype(qt_ref.dtype)
        k = _rope_full(_rms_rows(ak_ref[:, hs], gk_ref[...]), cos_a, sin_a)
        k_ref[:, hs] = k.astype(k_ref.dtype)
        vt_ref[h * VT_ROWS:h * VT_ROWS + dh, :] = av_ref[:, hs].T.astype(vt_ref.dtype)
        vt_ref[h * VT_ROWS + dh:(h + 1) * VT_ROWS, :] = jnp.ones((VT_ONES_ROWS, vt_ref.shape[1]), vt_ref.dtype)
    for c in range(IDX_HEADS * IDX_DIM // LANES):
        cs = slice(c * LANES, (c + 1) * LANES)
        iqt_ref[cs, :] = _rope_pair(iq_ref[:, cs], cos_i, sin_i).T.astype(iqt_ref.dtype)
    tail = tail_ref[...]
    ik_lo = _TAIL_START["ik"]
    ik_ref[...] = _rope_pair(tail, cos_i, sin_i)[:, ik_lo:ik_lo + IDX_DIM].astype(ik_ref.dtype)
    iw_lo = _TAIL_START["iw"]
    wt_ref[...] = (tail * (IDX_HEADS ** -0.5 * IDX_DIM ** -0.5)).T[iw_lo:iw_lo + IDX_HEADS, :]


def attention_prep(proj, tail, rope_a, rope_i, gq, gk, *, tm):
    s_len = proj.shape[0]
    iq_w = IDX_HEADS * IDX_DIM
    row = lambda width, blk: pl.BlockSpec((tm, width), lambda i: (i, blk))
    col = lambda height: pl.BlockSpec((height, tm), lambda i: (0, i))
    gain = pl.BlockSpec((1, ATT_HEAD_DIM), lambda i: (0, 0))
    return pl.pallas_call(
        _att_prep_kernel,
        out_shape=[
            jax.ShapeDtypeStruct((ATT_W, s_len), BF16),
            jax.ShapeDtypeStruct((s_len, ATT_W), BF16),
            jax.ShapeDtypeStruct((ATT_HEADS * VT_ROWS, s_len), BF16),
            jax.ShapeDtypeStruct((iq_w, s_len), BF16),
            jax.ShapeDtypeStruct((s_len, IDX_DIM), BF16),
            jax.ShapeDtypeStruct((IDX_HEADS, s_len), F32),
        ],
        grid=(s_len // tm,),
        in_specs=[
            row(ATT_W, _MAIN_START["aq"] // ATT_W), row(ATT_W, _MAIN_START["ak"] // ATT_W),
            row(ATT_W, _MAIN_START["av"] // ATT_W), row(iq_w, _MAIN_START["iq"] // iq_w),
            row(TAIL_W, 0), row(LANES, 0), row(LANES, 0), row(LANES, 0), row(LANES, 0), gain, gain,
        ],
        out_specs=[col(ATT_W), row(ATT_W, 0), col(ATT_HEADS * VT_ROWS), col(iq_w), row(IDX_DIM, 0), col(IDX_HEADS)],
        compiler_params=_cparams("parallel"),
        name="attention_prep",
    )(proj, proj, proj, proj, tail, rope_a[0], rope_a[1], rope_i[0], rope_i[1],
      gq.reshape(1, ATT_HEAD_DIM), gk.reshape(1, ATT_HEAD_DIM))


def _tail_weight(w_in_t, layer):
    rows = lambda name: w_in_t[layer, _SEG_START[name]:_SEG_START[name] + _SEG_SIZE[name], :]
    tail_used = sum(_SEG_SIZE[n] for n in _TAIL_ORDER)
    return jnp.concatenate([rows(n) for n in _TAIL_ORDER]
                           + [jnp.zeros((TAIL_W - tail_used, w_in_t.shape[2]), w_in_t.dtype)], axis=0)


def _tile(n, pref):
    t = min(n, pref)
    assert n % t == 0, (n, pref)
    return t


def _layer(x, layer, rope_a, rope_i, norm_mix, norm_ffn, w_in_t, att_q_norm, att_k_norm, mlstm_conv,
           mlstm_gate_bias, mlstm_out_norm, ret_out_norm, w_branch_att, w_branch_mlstm, w_branch_ret,
           w_out, w_ffn_in, w_ffn_out):
    s_len = x.shape[0]
    tm = _tile(s_len, 1024)
    h = rms_norm_bf16(x, norm_mix, tm=_tile(s_len, 512))
    proj = projection_main(h, w_in_t, layer, tm=_tile(s_len, 2048))
    tail = projection_tail(h, _tail_weight(w_in_t, layer), tm=_tile(s_len, 2048))

    def tseg(name):
        return tail[:, _TAIL_START[name]:_TAIL_START[name] + _SEG_SIZE[name]]

    tq = _tile(s_len, 256)
    qt, kk, vt, iqt, ik, wt = attention_prep(proj, tail, rope_a, rope_i, att_q_norm, att_k_norm, tm=tq)
    bias = select_bias(ik, iqt, wt, tq=tq, tk=_tile(s_len, 128))
    y_att = masked_attention(qt, kk, vt, bias, tq=tq, tk=_tile(s_len, 512))

    i_pre = tseg("mi") + mlstm_gate_bias[:MLSTM_HEADS]
    f_pre = tseg("mf") + mlstm_gate_bias[MLSTM_HEADS:]
    n_chunks = s_len // MLSTM_CHUNK
    logf = jax.nn.log_sigmoid(f_pre).reshape(n_chunks, MLSTM_CHUNK, MLSTM_HEADS)
    b_cum = jnp.cumsum(logf, axis=1).reshape(s_len, MLSTM_HEADS)
    gcol = jnp.concatenate([b_cum, i_pre], axis=-1)
    grow = gcol.reshape(n_chunks, MLSTM_CHUNK, 2 * MLSTM_HEADS).transpose(0, 2, 1)
    y_mlstm = mlstm_mixer(proj, gcol, grow, mlstm_out_norm.reshape(-1), mlstm_conv)

    y_ret = retention_mixer(proj, rope_a[0], rope_a[1], ret_out_norm.reshape(-1))

    merged = gated_merge(y_att, y_mlstm, y_ret, w_branch_att, w_branch_mlstm, w_branch_ret, layer, proj,
                         tm=tm, tn=512)
    x = residual_matmul(x, merged, w_out, layer, tm=_tile(s_len, 2048), tn=512)

    h_ffn = rms_norm_bf16(x, norm_ffn, tm=_tile(s_len, 512))
    act = swiglu(h_ffn, w_ffn_in, layer, tm=_tile(s_len, 2048), tn=512)
    x = residual_matmul(x, act, w_ffn_out, layer, tm=tm, tn=256)
    return x


def kernel(x, norm_mix, norm_ffn, w_in, att_q_norm, att_k_norm, mlstm_conv, mlstm_gate_bias,
           mlstm_out_norm, ret_out_norm, w_branch_att, w_branch_mlstm, w_branch_ret, w_out,
           w_ffn_in, w_ffn_out):
    b, s_len, d = x.shape
    assert d == D_MODEL and s_len % 256 == 0
    rope_a = _rope_tables(s_len, ATT_HEAD_DIM)
    rope_i = _rope_tables(s_len, IDX_DIM)
    w_in_t = jnp.swapaxes(w_in, 1, 2)
    outs = []
    for bi in range(b):
        xb = x[bi]
        for layer in range(w_in.shape[0]):
            xb = _layer(xb, layer, rope_a, rope_i, norm_mix[layer], norm_ffn[layer], w_in_t, att_q_norm[layer],
                        att_k_norm[layer], mlstm_conv[layer], mlstm_gate_bias[layer],
                        mlstm_out_norm[layer], ret_out_norm[layer], w_branch_att, w_branch_mlstm,
                        w_branch_ret, w_out, w_ffn_in, w_ffn_out)
        outs.append(xb)
    return jnp.stack(outs, axis=0)
```

```python
import functools
import math

import jax
import jax.numpy as jnp
import numpy as np
from jax import lax
from jax.experimental import pallas as pl
from jax.experimental.pallas import tpu as pltpu

F32 = jnp.float32
BF16 = jnp.bfloat16

D_MODEL = 2048
ATT_HEADS = 8
ATT_HEAD_DIM = 128
IDX_HEADS = 8
IDX_DIM = 64
INDEX_TOPK = 256
MLSTM_HEADS = 4
MLSTM_QK_DIM = 128
MLSTM_V_DIM = 256
MLSTM_CHUNK = 512
CONV_WIDTH = 4
RET_HEADS = 8
RET_QK_DIM = 128
RET_V_DIM = 128
RET_CHUNK = 512
ROPE_THETA = 10000.0
NORM_EPS = 1e-6
FFN_DIM = 5632
N_BRANCHES = 3

ATT_W = ATT_HEADS * ATT_HEAD_DIM
MLSTM_QK_W = MLSTM_HEADS * MLSTM_QK_DIM
MLSTM_V_W = MLSTM_HEADS * MLSTM_V_DIM
RET_W = RET_HEADS * RET_V_DIM

_SEGMENTS = (
    ("aq", ATT_W), ("ak", ATT_W), ("av", ATT_W), ("iq", IDX_HEADS * IDX_DIM), ("ik", IDX_DIM),
    ("iw", IDX_HEADS), ("mq", MLSTM_QK_W), ("mk", MLSTM_QK_W), ("mv", MLSTM_V_W),
    ("mi", MLSTM_HEADS), ("mf", MLSTM_HEADS), ("mo", MLSTM_V_W), ("rq", RET_W), ("rk", RET_W),
    ("rv", RET_W), ("rg", RET_W), ("gates", N_BRANCHES * D_MODEL),
)
_SEG_SIZE = dict(_SEGMENTS)
_SEG_START = {}
_off = 0
for _name, _size in _SEGMENTS:
    _SEG_START[_name] = _off
    _off += _size

_MAIN_ORDER = ("aq", "ak", "av", "mv", "mo", "rq", "rk", "rv", "rg", "gates", "iq", "mq", "mk")
_TAIL_ORDER = ("ik", "iw", "mi", "mf")
_MAIN_START = {}
_off = 0
for _name in _MAIN_ORDER:
    _MAIN_START[_name] = _off
    _off += _SEG_SIZE[_name]
MAIN_W = _off
_TAIL_START = {}
_off = 0
for _name in _TAIL_ORDER:
    _TAIL_START[_name] = _off
    _off += _SEG_SIZE[_name]
LANES = 128
TAIL_W = LANES

VMEM_LIMIT = 56 * 1024 * 1024
MASK_VALUE = -1e30


def _cparams(*semantics):
    return pltpu.CompilerParams(dimension_semantics=semantics, vmem_limit_bytes=VMEM_LIMIT)


def _rms_rows(x, g):
    ms = jnp.mean(x * x, axis=-1, keepdims=True)
    return x * lax.rsqrt(ms + NORM_EPS) * g


def _rms_norm_kernel(x_ref, g_ref, o_ref):
    o_ref[...] = _rms_rows(x_ref[...], g_ref[...]).astype(o_ref.dtype)


def rms_norm_bf16(x, g, *, tm):
    m, k = x.shape
    return pl.pallas_call(
        _rms_norm_kernel,
        out_shape=jax.ShapeDtypeStruct((m, k), BF16),
        grid=(m // tm,),
        in_specs=[pl.BlockSpec((tm, k), lambda i: (i, 0)), pl.BlockSpec((1, k), lambda i: (0, 0))],
        out_specs=pl.BlockSpec((tm, k), lambda i: (i, 0)),
        compiler_params=_cparams("parallel"),
        name="rms_norm",
    )(x, g.reshape(1, k))


PROJ_TN = 512


def _matmul_nt_kernel(*refs):
    a_ref, wt_ref, o_ref = refs[-3:]
    wt = wt_ref[...].reshape(wt_ref.shape[-2:])
    o_ref[...] = lax.dot_general(a_ref[...], wt.astype(BF16), (((1,), (1,)), ((), ())),
                                 preferred_element_type=F32)


def projection_main(h, w_in_t, layer, *, tm):
    m, k = h.shape
    rows = []
    for name in _MAIN_ORDER:
        assert _SEG_SIZE[name] % PROJ_TN == 0 and _SEG_START[name] % 8 == 0
        rows += [_SEG_START[name] + c for c in range(0, _SEG_SIZE[name], PROJ_TN)]
    src = jnp.asarray([r // 8 for r in rows], jnp.int32)
    grid_spec = pltpu.PrefetchScalarGridSpec(
        num_scalar_prefetch=1,
        grid=(m // tm, len(rows)),
        in_specs=[
            pl.BlockSpec((tm, k), lambda i, j, src: (i, 0)),
            pl.BlockSpec((pl.Element(1), pl.Element(PROJ_TN), pl.Element(k)),
                         lambda i, j, src: (layer, src[j] * 8, 0)),
        ],
        out_specs=pl.BlockSpec((tm, PROJ_TN), lambda i, j, src: (i, j)),
    )
    return pl.pallas_call(
        _matmul_nt_kernel,
        out_shape=jax.ShapeDtypeStruct((m, MAIN_W), F32),
        grid_spec=grid_spec,
        compiler_params=_cparams("parallel", "arbitrary"),
        name="projection_main",
    )(src, h, w_in_t)


def projection_tail(h, w_tail_t, *, tm):
    m, k = h.shape
    return pl.pallas_call(
        _matmul_nt_kernel,
        out_shape=jax.ShapeDtypeStruct((m, TAIL_W), F32),
        grid=(m // tm,),
        in_specs=[pl.BlockSpec((tm, k), lambda i: (i, 0)), pl.BlockSpec((TAIL_W, k), lambda i: (0, 0))],
        out_specs=pl.BlockSpec((tm, TAIL_W), lambda i: (i, 0)),
        compiler_params=_cparams("parallel"),
        name="projection_tail",
    )(h, w_tail_t)


def _swiglu_kernel(h_ref, wg_ref, wu_ref, o_ref):
    h = h_ref[...]
    gate = jnp.dot(h, wg_ref[...].astype(BF16), preferred_element_type=F32)
    up = jnp.dot(h, wu_ref[...].astype(BF16), preferred_element_type=F32)
    o_ref[...] = (gate * jax.nn.sigmoid(gate) * up).astype(o_ref.dtype)


def swiglu(h, w_in_stack, layer, *, tm, tn):
    m, k = h.shape
    f = w_in_stack.shape[2] // 2
    nj = f // tn
    w_in = w_in_stack
    return pl.pallas_call(
        _swiglu_kernel,
        out_shape=jax.ShapeDtypeStruct((m, f), BF16),
        grid=(m // tm, nj),
        in_specs=[
            pl.BlockSpec((tm, k), lambda i, j: (i, 0)),
            _layer_weight_spec(layer, k, tn),
            _layer_weight_spec(layer, k, tn, col_offset_blocks=nj),
        ],
        out_specs=pl.BlockSpec((tm, tn), lambda i, j: (i, j)),
        compiler_params=_cparams("parallel", "arbitrary"),
        name="swiglu",
    )(h, w_in, w_in)


def _residual_matmul_kernel(r_ref, a_ref, w_ref, o_ref):
    o_ref[...] = r_ref[...] + jnp.dot(a_ref[...], w_ref[...].astype(BF16), preferred_element_type=F32)


def _layer_weight_spec(layer, k, tn, col_offset_blocks=0):
    return pl.BlockSpec((None, k, tn), lambda i, j: (layer, 0, j + col_offset_blocks))


def residual_matmul(res, a, w_stack, layer, *, tm, tn, single_buffer_a=False):
    m, k = a.shape
    n = w_stack.shape[2]
    a_mode = dict(pipeline_mode=pl.Buffered(1)) if single_buffer_a else {}
    return pl.pallas_call(
        _residual_matmul_kernel,
        out_shape=jax.ShapeDtypeStruct((m, n), F32),
        grid=(m // tm, n // tn),
        in_specs=[
            pl.BlockSpec((tm, tn), lambda i, j: (i, j)),
            pl.BlockSpec((tm, k), lambda i, j: (i, 0), **a_mode),
            _layer_weight_spec(layer, k, tn),
        ],
        out_specs=pl.BlockSpec((tm, tn), lambda i, j: (i, j)),
        compiler_params=_cparams("parallel", "arbitrary"),
        name="residual_matmul",
    )(res, a, w_stack)


def _merge_kernel(ya_ref, ym_ref, yr_ref, wa_ref, wm_ref, wr_ref, ga_ref, gm_ref, gr_ref, o_ref):
    def branch(y_ref, w_ref, g_ref):
        proj = jnp.dot(y_ref[...], w_ref[...].astype(BF16), preferred_element_type=F32)
        return jax.nn.sigmoid(g_ref[...]) * proj

    merged = branch(ya_ref, wa_ref, ga_ref) + branch(ym_ref, wm_ref, gm_ref) + branch(yr_ref, wr_ref, gr_ref)
    o_ref[...] = merged.astype(o_ref.dtype)


def gated_merge(y_att, y_mlstm, y_ret, w_att, w_mlstm, w_ret, layer, proj, *, tm, tn):
    m, k = y_att.shape
    n = w_att.shape[2]
    gate_blk = _MAIN_START["gates"] // tn
    per_branch = n // tn
    y_spec = pl.BlockSpec((tm, k), lambda i, j: (i, 0))
    w_spec = _layer_weight_spec(layer, k, tn)

    def g_spec(b):
        return pl.BlockSpec((tm, tn), lambda i, j: (i, gate_blk + b * per_branch + j))

    return pl.pallas_call(
        _merge_kernel,
        out_shape=jax.ShapeDtypeStruct((m, n), BF16),
        grid=(m // tm, n // tn),
        in_specs=[y_spec, y_spec, y_spec, w_spec, w_spec, w_spec, g_spec(0), g_spec(1), g_spec(2)],
        out_specs=pl.BlockSpec((tm, tn), lambda i, j: (i, j)),
        compiler_params=_cparams("parallel", "arbitrary"),
        name="gated_merge",
    )(y_att, y_mlstm, y_ret, w_att, w_mlstm, w_ret, proj, proj, proj)


INT_MIN = -(2 ** 31)


def _sortable(x):
    bits = pltpu.bitcast(x, jnp.int32)
    bits = jnp.where(bits == INT_MIN, 0, bits)
    return bits ^ ((bits >> 31) & 0x7FFFFFFF)


GROUP = 256


def _bit_transpose32(words):
    a = list(words)
    j, m = 16, 0x0000FFFF
    while j:
        k = 0
        while k < 32:
            t = (a[k] ^ lax.shift_right_logical(a[k + j], jnp.int32(j))) & m
            a[k] = a[k] ^ t
            a[k + j] = a[k + j] ^ (t << j)
            k = (k + j + 1) & ~j
        j >>= 1
        m = (m ^ (m << j)) & 0xFFFFFFFF
    return a


def _select_kernel(ik_ref, iqt_ref, wt_ref, bias_ref, key_ref, plane_ref, alive_ref, *, tq, tk, topk):
    s_len = ik_ref.shape[0]
    i = pl.program_id(0)
    q_lo = i * tq
    n_groups = (q_lo + tq + GROUP - 1) // GROUP
    n_chunks = n_groups * (GROUP // tk)
    q_pos = q_lo + lax.broadcasted_iota(jnp.int32, (tk, tq), 1)
    row_iota = lax.broadcasted_iota(jnp.int32, (tk, tq), 0)

    @pl.when(i == 0)
    def _():
        plane_ref[...] = jnp.zeros(plane_ref.shape, jnp.int32)

    def score_group(g, carry, *, diagonal):
        g0 = pl.multiple_of(g * GROUP, GROUP)
        slabs = []
        for sub in range(GROUP // tk):
            k0 = g0 + sub * tk
            ik = ik_ref[pl.ds(k0, tk), :]
            acc = jnp.zeros((tk, tq), F32)
            for h in range(IDX_HEADS):
                lg = jnp.dot(ik, iqt_ref[h * IDX_DIM:(h + 1) * IDX_DIM, :], preferred_element_type=F32)
                acc = acc + wt_ref[h:h + 1, :] * jnp.maximum(lg, 0.0)
            key = _sortable(acc)
            if diagonal:
                key = jnp.where(k0 + row_iota <= q_pos, key, INT_MIN)
            key_ref[pl.ds(k0, tk), :] = key
            slabs += [key[8 * j:8 * j + 8, :] for j in range(tk // 8)]
        planes = _bit_transpose32(slabs)
        r0 = pl.multiple_of(g * 8, 8)
        for p in range(32):
            word = planes[31 - p]
            plane_ref[p, pl.ds(r0, 8), :] = ~word if p == 31 else word
        return carry

    assert tq <= GROUP
    lax.fori_loop(0, n_groups - 1, functools.partial(score_group, diagonal=False), 0)
    score_group(n_groups - 1, 0, diagonal=True)

    grp = lax.broadcasted_iota(jnp.int32, alive_ref.shape, 0) >> 3
    alive_ref[...] = jnp.where(grp < n_groups, -1, 0)
    n_gt = jnp.zeros((1, tq), jnp.int32)
    thr_u = jnp.zeros((1, tq), jnp.int32)
    for p in range(31, -1, -1):
        alive = alive_ref[...]
        ones = alive & plane_ref[p]
        c = jnp.sum(lax.population_count(ones), axis=0, keepdims=True)
        take1 = (n_gt + c) >= topk
        alive_ref[...] = jnp.where(take1, ones, alive ^ ones)
        n_gt = n_gt + jnp.where(take1, 0, c)
        thr_u = thr_u | jnp.where(take1, INT_MIN if p == 31 else (1 << p), 0)
    thr = thr_u ^ INT_MIN
    n_eq = jnp.sum(lax.population_count(alive_ref[...]), axis=0, keepdims=True)

    need = topk - n_gt
    excess = jnp.max(n_eq - need)

    def count(pred):
        def body(c, acc):
            k0 = pl.multiple_of(c * tk, tk)
            hit = pred(key_ref[pl.ds(k0, tk), :], k0).astype(jnp.int32)
            return acc + jnp.sum(hit.reshape(tk // 8, 8, tq), axis=0)

        acc = lax.fori_loop(0, n_chunks, body, jnp.zeros((8, tq), jnp.int32))
        return jnp.sum(acc, axis=0, keepdims=True)

    def tie_cut():
        cut = jnp.zeros((1, tq), jnp.int32)
        n_bits = max(1, int(math.ceil(math.log2(s_len))))
        for bit in range(n_bits - 1, -1, -1):
            cand = cut | (1 << bit)
            n_before = count(lambda key, k0, cand=cand: (key == thr) & (k0 + row_iota < cand))
            cut = jnp.where(n_before < need, cand, cut)
        return cut

    cut = lax.cond(excess > 0, tie_cut, lambda: jnp.full((1, tq), s_len, jnp.int32))

    def bias_chunk(c, carry):
        k0 = pl.multiple_of(c * tk, tk)
        key = key_ref[pl.ds(k0, tk), :]
        k_pos = k0 + row_iota
        sel = (key > thr) | ((key == thr) & (k_pos <= cut))
        sel = sel & (k_pos <= q_pos)
        bias_ref[pl.ds(k0, tk), :] = jnp.where(sel, 0.0, MASK_VALUE).astype(bias_ref.dtype)
        return carry

    lax.fori_loop(0, n_chunks, bias_chunk, 0)

    def fill_chunk(c, carry):
        k0 = pl.multiple_of(c * tk, tk)
        bias_ref[pl.ds(k0, tk), :] = jnp.full((tk, tq), MASK_VALUE, bias_ref.dtype)
        return carry

    lax.fori_loop(n_chunks, s_len // tk, fill_chunk, 0)


def select_bias(ik, iqt, wt, *, tq, tk):
    s_len = ik.shape[0]
    topk = min(INDEX_TOPK, s_len // 4)
    kern = functools.partial(_select_kernel, tq=tq, tk=tk, topk=topk)
    return pl.pallas_call(
        kern,
        out_shape=jax.ShapeDtypeStruct((s_len, s_len), BF16),
        grid=(s_len // tq,),
        in_specs=[
            pl.BlockSpec((s_len, IDX_DIM), lambda i: (0, 0)),
            pl.BlockSpec((IDX_HEADS * IDX_DIM, tq), lambda i: (0, i)),
            pl.BlockSpec((IDX_HEADS, tq), lambda i: (0, i)),
        ],
        out_specs=pl.BlockSpec((s_len, tq), lambda i: (0, i)),
        scratch_shapes=[
            pltpu.VMEM((s_len, tq), jnp.int32),
            pltpu.VMEM((32, s_len // 32, tq), jnp.int32),
            pltpu.VMEM((s_len // 32, tq), jnp.int32),
        ],
        compiler_params=_cparams("arbitrary"),
        name="select_bias",
    )(ik, iqt, wt)


ATT_Q_SCALE = ATT_HEAD_DIM ** -0.5 * math.log2(math.e)


VT_ONES_ROWS = 16
VT_ROWS = ATT_HEAD_DIM + VT_ONES_ROWS


def _attention_kernel(qi_ref, ki_ref, qt_ref, k_ref, vt_ref, bias_ref, o_ref, m_ref, acc_ref,
                      *s_refs, tq, tk):
    step = pl.program_id(0)
    i = qi_ref[step]
    kb = ki_ref[step]
    last_kb = (i * tq + tq - 1) // tk
    dh = ATT_HEAD_DIM

    @pl.when(kb == 0)
    def _():
        m_ref[...] = jnp.full(m_ref.shape, MASK_VALUE, F32)
        acc_ref[...] = jnp.zeros(acc_ref.shape, F32)

    bias = bias_ref[...].astype(F32)

    m_new = []
    for h in range(ATT_HEADS):
        hs = slice(h * dh, (h + 1) * dh)
        s = jnp.dot(k_ref[:, hs], qt_ref[hs, :], preferred_element_type=F32) + bias
        s_refs[h][0] = s
        m_new.append(jnp.maximum(m_ref[h], jnp.max(s, axis=0, keepdims=True)))
    zero = lax.shift_right_logical(kb, 30)
    for h in range(ATT_HEADS):
        alpha = jnp.exp2(m_ref[h] - m_new[h])
        p = jnp.exp2(s_refs[h][zero] - m_new[h])
        acc_ref[h] = alpha * acc_ref[h] + jnp.dot(vt_ref[h * VT_ROWS:(h + 1) * VT_ROWS, :],
                                                  p.astype(BF16), preferred_element_type=F32)
        m_ref[h] = m_new[h]

    @pl.when(kb == last_kb)
    def _():
        for h in range(ATT_HEADS):
            out_t = acc_ref[h, :dh, :] / acc_ref[h, dh:dh + 1, :]
            o_ref[:, h * dh:(h + 1) * dh] = out_t.T.astype(o_ref.dtype)


def masked_attention(qt, k, vt, bias, *, tq, tk):
    width, s_len = qt.shape
    pairs = [(i, kb) for i in range(s_len // tq) for kb in range((i * tq + tq - 1) // tk + 1)]
    qi = jnp.asarray([p[0] for p in pairs], jnp.int32)
    ki = jnp.asarray([p[1] for p in pairs], jnp.int32)
    grid_spec = pltpu.PrefetchScalarGridSpec(
        num_scalar_prefetch=2,
        grid=(len(pairs),),
        in_specs=[
            pl.BlockSpec((width, tq), lambda s, qi, ki: (0, qi[s])),
            pl.BlockSpec((tk, width), lambda s, qi, ki: (ki[s], 0)),
            pl.BlockSpec((ATT_HEADS * VT_ROWS, tk), lambda s, qi, ki: (0, ki[s])),
            pl.BlockSpec((tk, tq), lambda s, qi, ki: (ki[s], qi[s])),
        ],
        out_specs=pl.BlockSpec((tq, width), lambda s, qi, ki: (qi[s], 0)),
        scratch_shapes=[
            pltpu.VMEM((ATT_HEADS, 1, tq), F32),
            pltpu.VMEM((ATT_HEADS, VT_ROWS, tq), F32),
        ] + [pltpu.VMEM((1, tk, tq), F32) for _ in range(ATT_HEADS)],
    )
    return pl.pallas_call(
        functools.partial(_attention_kernel, tq=tq, tk=tk),
        out_shape=jax.ShapeDtypeStruct((s_len, width), BF16),
        grid_spec=grid_spec,
        compiler_params=_cparams("arbitrary"),
        name="masked_attention",
    )(qi, ki, qt, k, vt, bias)


HALO = 8


def _mlstm_kernel(q_ref, k_ref, v_ref, o_ref, gcol_ref, grow_ref, gain_ref, cw_ref, y_ref,
                  c_ref, n_ref, m_ref, halo_ref, qk_ref, qc_ref):
    L = q_ref.shape[0]
    dk, dv, nh = MLSTM_QK_DIM, MLSTM_V_DIM, MLSTM_HEADS
    qk_w = nh * dk

    @pl.when(pl.program_id(0) == 0)
    def _():
        c_ref[...] = jnp.zeros(c_ref.shape, F32)
        n_ref[...] = jnp.zeros(n_ref.shape, F32)
        m_ref[...] = jnp.zeros(m_ref.shape, F32)
        halo_ref[...] = jnp.zeros(halo_ref.shape, F32)

    def causal_conv(x_ref, slot, taps):
        x = x_ref[...]
        ext = jnp.concatenate([halo_ref[slot], x], axis=0)
        out = x * taps[CONV_WIDTH - 1:CONV_WIDTH, :]
        for j in range(CONV_WIDTH - 1):
            lo = HALO - (CONV_WIDTH - 1 - j)
            out = out + ext[lo:lo + L, :] * taps[j:j + 1, :]
        halo_ref[slot] = x[L - HALO:, :]
        return out

    q_all = causal_conv(q_ref, 0, cw_ref[:, :qk_w])
    k_all = causal_conv(k_ref, 1, cw_ref[:, qk_w:])

    tril = lax.broadcasted_iota(jnp.int32, (L, L), 1) <= lax.broadcasted_iota(jnp.int32, (L, L), 0)
    gcol = gcol_ref[...]
    grow = grow_ref[0]
    stats = []
    for h in range(nh):
        q = q_all[:, h * dk:(h + 1) * dk]
        k = k_all[:, h * dk:(h + 1) * dk] * (dk ** -0.5)
        b_col = gcol[:, h:h + 1]
        i_col = gcol[:, nh + h:nh + h + 1]
        b_row = grow[h:h + 1, :]
        i_row = grow[nh + h:nh + h + 1, :]
        m_prev = m_ref[h]
        c_prev = c_ref[h]
        n_prev = n_ref[h]
        qb = q.astype(BF16)
        vb = v_ref[:, h * dv:(h + 1) * dv].astype(BF16)
        qk_ref[h] = lax.dot_general(qb, k.astype(BF16), (((1,), (1,)), ((), ())),
                                    preferred_element_type=F32)
        qc_ref[h] = jnp.dot(qb, c_prev.astype(BF16), preferred_element_type=F32)
        qn = jnp.sum(q * n_prev, axis=-1, keepdims=True)

        b_end = b_row[:, L - 1:L]
        end_row = b_end - b_row + i_row
        m_new = jnp.maximum(b_end + m_prev, jnp.max(end_row, axis=-1, keepdims=True))
        decay = jnp.exp(b_end + m_prev - m_new)
        end_col = b_end - b_col + i_col
        ka = k * jnp.exp(end_col - m_new)
        c_ref[h] = decay * c_prev + jnp.dot(ka.T.astype(BF16), vb, preferred_element_type=F32)
        n_ref[h] = decay * n_prev + jnp.sum(ka, axis=0, keepdims=True)
        m_ref[h] = m_new
        stats.append((b_col, b_row, i_row, m_prev, qn))

    for h in range(nh):
        b_col, b_row, i_row, m_prev, qn = stats[h]
        vb = v_ref[:, h * dv:(h + 1) * dv].astype(BF16)
        log_intra = jnp.where(tril, b_col - b_row + i_row, -jnp.inf)
        m_inter = b_col + m_prev
        m_t = jnp.maximum(m_inter, jnp.max(log_intra, axis=-1, keepdims=True))
        w_inter = jnp.exp(m_inter - m_t)
        p = jnp.exp(log_intra - m_t) * qk_ref[h]
        num = w_inter * qc_ref[h] + jnp.dot(p.astype(BF16), vb, preferred_element_type=F32)
        den = w_inter * qn + jnp.sum(p, axis=-1, keepdims=True)
        h_out = num / jnp.maximum(jnp.abs(den), jnp.exp(-m_t))

        gain = gain_ref[:, h * dv:(h + 1) * dv]
        normed = h_out * lax.rsqrt(jnp.mean(h_out * h_out, axis=-1, keepdims=True) + NORM_EPS) * gain
        y_ref[:, h * dv:(h + 1) * dv] = (normed * jax.nn.sigmoid(o_ref[:, h * dv:(h + 1) * dv])).astype(y_ref.dtype)


def mlstm_mixer(proj, gcol, grow, gain, conv_w):
    s_len = proj.shape[0]
    L = MLSTM_CHUNK
    q_blk = _MAIN_START["mq"] // MLSTM_QK_W
    k_blk = _MAIN_START["mk"] // MLSTM_QK_W
    v_blk = _MAIN_START["mv"] // MLSTM_V_W
    o_blk = _MAIN_START["mo"] // MLSTM_V_W
    return pl.pallas_call(
        _mlstm_kernel,
        out_shape=jax.ShapeDtypeStruct((s_len, MLSTM_V_W), BF16),
        grid=(s_len // L,),
        in_specs=[
            pl.BlockSpec((L, MLSTM_QK_W), lambda c: (c, q_blk)),
            pl.BlockSpec((L, MLSTM_QK_W), lambda c: (c, k_blk)),
            pl.BlockSpec((L, MLSTM_V_W), lambda c: (c, v_blk)),
            pl.BlockSpec((L, MLSTM_V_W), lambda c: (c, o_blk)),
            pl.BlockSpec((L, 2 * MLSTM_HEADS), lambda c: (c, 0)),
            pl.BlockSpec((1, 2 * MLSTM_HEADS, L), lambda c: (c, 0, 0)),
            pl.BlockSpec((1, MLSTM_V_W), lambda c: (0, 0)),
            pl.BlockSpec((CONV_WIDTH, 2 * MLSTM_QK_W), lambda c: (0, 0)),
        ],
        out_specs=pl.BlockSpec((L, MLSTM_V_W), lambda c: (c, 0)),
        scratch_shapes=[
            pltpu.VMEM((MLSTM_HEADS, MLSTM_QK_DIM, MLSTM_V_DIM), F32),
            pltpu.VMEM((MLSTM_HEADS, 1, MLSTM_QK_DIM), F32),
            pltpu.VMEM((MLSTM_HEADS, 1, 1), F32),
            pltpu.VMEM((2, HALO, MLSTM_QK_W), F32),
            pltpu.VMEM((MLSTM_HEADS, L, L), F32),
            pltpu.VMEM((MLSTM_HEADS, L, MLSTM_V_DIM), F32),
        ],
        compiler_params=_cparams("arbitrary"),
        name="mlstm_mixer",
    )(proj, proj, proj, proj, gcol, grow, gain.reshape(1, MLSTM_V_W), conv_w)


def _rope_full(x, cos2, sin2):
    return x * cos2 + pltpu.roll(x, x.shape[-1] // 2, 1) * sin2


def _retention_kernel(q_ref, k_ref, v_ref, g_ref, cos_ref, sin_ref, dint_ref, qdec_ref, kdec_ref,
                      cdec_ref, gain_ref, y_ref, r_ref, qk_ref, cross_ref):
    dk, dv = RET_QK_DIM, RET_V_DIM

    @pl.when(pl.program_id(0) == 0)
    def _():
        r_ref[...] = jnp.zeros(r_ref.shape, F32)

    cos2 = cos_ref[...]
    sin2 = sin_ref[...]
    for h in range(RET_HEADS):
        q = _rope_full(q_ref[:, h * dk:(h + 1) * dk], cos2, sin2)
        k = _rope_full(k_ref[:, h * dk:(h + 1) * dk], cos2, sin2) * (dk ** -0.5)
        vb = v_ref[:, h * dv:(h + 1) * dv].astype(BF16)
        r_prev = r_ref[h]
        qk_ref[h] = lax.dot_general(q.astype(BF16), k.astype(BF16), (((1,), (1,)), ((), ())),
                                    preferred_element_type=F32)
        cross_ref[h] = jnp.dot((q * qdec_ref[h]).astype(BF16), r_prev.astype(BF16),
                               preferred_element_type=F32)
        kd = k * kdec_ref[h]
        r_ref[h] = cdec_ref[h] * r_prev + jnp.dot(kd.T.astype(BF16), vb, preferred_element_type=F32)

    for h in range(RET_HEADS):
        vb = v_ref[:, h * dv:(h + 1) * dv].astype(BF16)
        inner = jnp.dot((qk_ref[h] * dint_ref[h]).astype(BF16), vb, preferred_element_type=F32)
        out = inner + cross_ref[h]

        mu = jnp.mean(out, axis=-1, keepdims=True)
        xc = out - mu
        normed = xc * lax.rsqrt(jnp.mean(xc * xc, axis=-1, keepdims=True) + NORM_EPS)
        gate = g_ref[:, h * dv:(h + 1) * dv]
        y = normed * gain_ref[:, h * dv:(h + 1) * dv] * (gate * jax.nn.sigmoid(gate))
        y_ref[:, h * dv:(h + 1) * dv] = y.astype(y_ref.dtype)


def _retention_tables(size):
    log_gamma = jnp.log(1.0 - 2.0 ** (-5.0 - jnp.arange(RET_HEADS, dtype=F32)))
    pos = jnp.arange(size, dtype=F32)
    rel = pos[:, None] - pos[None, :]
    decay_intra = jnp.where(rel >= 0, jnp.exp(log_gamma[:, None, None] * jnp.maximum(rel, 0.0)), 0.0)
    q_decay = jnp.exp(log_gamma[:, None] * (pos + 1.0))[:, :, None]
    k_decay = jnp.exp(log_gamma[:, None] * (size - 1.0 - pos))[:, :, None]
    chunk_decay = jnp.exp(log_gamma * size)[:, None, None]
    return decay_intra, q_decay, k_decay, chunk_decay


def retention_mixer(proj, cos2, sin2, gain):
    s_len = proj.shape[0]
    L = RET_CHUNK
    q_blk = _MAIN_START["rq"] // RET_W
    k_blk = _MAIN_START["rk"] // RET_W
    v_blk = _MAIN_START["rv"] // RET_W
    g_blk = _MAIN_START["rg"] // RET_W
    decay_intra, q_decay, k_decay, chunk_decay = _retention_tables(L)
    nh = RET_HEADS
    return pl.pallas_call(
        _retention_kernel,
        out_shape=jax.ShapeDtypeStruct((s_len, RET_W), BF16),
        grid=(s_len // L,),
        in_specs=[
            pl.BlockSpec((L, RET_W), lambda c: (c, q_blk)),
            pl.BlockSpec((L, RET_W), lambda c: (c, k_blk)),
            pl.BlockSpec((L, RET_W), lambda c: (c, v_blk)),
            pl.BlockSpec((L, RET_W), lambda c: (c, g_blk)),
            pl.BlockSpec((L, RET_QK_DIM), lambda c: (c, 0)),
            pl.BlockSpec((L, RET_QK_DIM), lambda c: (c, 0)),
            pl.BlockSpec((nh, L, L), lambda c: (0, 0, 0)),
            pl.BlockSpec((nh, L, 1), lambda c: (0, 0, 0)),
            pl.BlockSpec((nh, L, 1), lambda c: (0, 0, 0)),
            pl.BlockSpec((nh, 1, 1), lambda c: (0, 0, 0)),
            pl.BlockSpec((1, RET_W), lambda c: (0, 0)),
        ],
        out_specs=pl.BlockSpec((L, RET_W), lambda c: (c, 0)),
        scratch_shapes=[
            pltpu.VMEM((nh, RET_QK_DIM, RET_V_DIM), F32),
            pltpu.VMEM((nh, L, L), F32),
            pltpu.VMEM((nh, L, RET_V_DIM), F32),
        ],
        compiler_params=_cparams("arbitrary"),
        name="retention_mixer",
    )(proj, proj, proj, proj, cos2, sin2, decay_intra, q_decay, k_decay, chunk_decay,
      gain.reshape(1, RET_W))


def _rope_tables(s_len, d):
    half = d // 2
    inv_freq = ROPE_THETA ** (-jnp.arange(half, dtype=F32) * 2.0 / d)
    ang = jnp.arange(s_len, dtype=jnp.int32).astype(F32)[:, None] * inv_freq[None, :]
    cos, sin = jnp.cos(ang), jnp.sin(ang)
    reps = LANES // d
    return jnp.tile(jnp.concatenate([cos, cos], -1), (1, reps)), jnp.tile(jnp.concatenate([-sin, sin], -1), (1, reps))


def _rope_pair(x, cos4, sin4):
    half = IDX_DIM // 2
    lane = lax.broadcasted_iota(jnp.int32, x.shape, 1)
    partner = jnp.where((lane % IDX_DIM) < half, pltpu.roll(x, LANES - half, 1), pltpu.roll(x, half, 1))
    return x * cos4 + partner * sin4


def _att_prep_kernel(aq_ref, ak_ref, av_ref, iq_ref, tail_ref, cosa_ref, sina_ref, cosi_ref, sini_ref,
                     gq_ref, gk_ref, qt_ref, k_ref, vt_ref, iqt_ref, ik_ref, wt_ref):
    dh = ATT_HEAD_DIM
    cos_a, sin_a = cosa_ref[...], sina_ref[...]
    cos_i, sin_i = cosi_ref[...], sini_ref[...]
    for h in range(ATT_HEADS):
        hs = slice(h * dh, (h + 1) * dh)
        q = _rope_full(_rms_rows(aq_ref[:, hs], gq_ref[...]), cos_a, sin_a)
        qt_ref[hs, :] = (q * ATT_Q_SCALE).T.astype(qt_ref.dtype)
        k = _rope_full(_rms_rows(ak_ref[:, hs], gk_ref[...]), cos_a, sin_a)
        k_ref[:, hs] = k.astype(k_ref.dtype)
        vt_ref[h * VT_ROWS:h * VT_ROWS + dh, :] = av_ref[:, hs].T.astype(vt_ref.dtype)
        vt_ref[h * VT_ROWS + dh:(h + 1) * VT_ROWS, :] = jnp.ones((VT_ONES_ROWS, vt_ref.shape[1]), vt_ref.dtype)
    for c in range(IDX_HEADS * IDX_DIM // LANES):
        cs = slice(c * LANES, (c + 1) * LANES)
        iqt_ref[cs, :] = _rope_pair(iq_ref[:, cs], cos_i, sin_i).T.astype(iqt_ref.dtype)
    tail = tail_ref[...]
    ik_lo = _TAIL_START["ik"]
    ik_ref[...] = _rope_pair(tail, cos_i, sin_i)[:, ik_lo:ik_lo + IDX_DIM].astype(ik_ref.dtype)
    iw_lo = _TAIL_START["iw"]
    wt_ref[...] = (tail * (IDX_HEADS ** -0.5 * IDX_DIM ** -0.5)).T[iw_lo:iw_lo + IDX_HEADS, :]


def attention_prep(proj, tail, rope_a, rope_i, gq, gk, *, tm):
    s_len = proj.shape[0]
    iq_w = IDX_HEADS * IDX_DIM
    row = lambda width, blk: pl.BlockSpec((tm, width), lambda i: (i, blk))
    col = lambda height: pl.BlockSpec((height, tm), lambda i: (0, i))
    gain = pl.BlockSpec((1, ATT_HEAD_DIM), lambda i: (0, 0))
    return pl.pallas_call(
        _att_prep_kernel,
        out_shape=[
            jax.ShapeDtypeStruct((ATT_W, s_len), BF16),
            jax.ShapeDtypeStruct((s_len, ATT_W), BF16),
            jax.ShapeDtypeStruct((ATT_HEADS * VT_ROWS, s_len), BF16),
            jax.ShapeDtypeStruct((iq_w, s_len), BF16),
            jax.ShapeDtypeStruct((s_len, IDX_DIM), BF16),
            jax.ShapeDtypeStruct((IDX_HEADS, s_len), F32),
        ],
        grid=(s_len // tm,),
        in_specs=[
            row(ATT_W, _MAIN_START["aq"] // ATT_W), row(ATT_W, _MAIN_START["ak"] // ATT_W),
            row(ATT_W, _MAIN_START["av"] // ATT_W), row(iq_w, _MAIN_START["iq"] // iq_w),
            row(TAIL_W, 0), row(LANES, 0), row(LANES, 0), row(LANES, 0), row(LANES, 0), gain, gain,
        ],
        out_specs=[col(ATT_W), row(ATT_W, 0), col(ATT_HEADS * VT_ROWS), col(iq_w), row(IDX_DIM, 0), col(IDX_HEADS)],
        compiler_params=_cparams("parallel"),
        name="attention_prep",
    )(proj, proj, proj, proj, tail, rope_a[0], rope_a[1], rope_i[0], rope_i[1],
      gq.reshape(1, ATT_HEAD_DIM), gk.reshape(1, ATT_HEAD_DIM))


def _tail_weight(w_in_t, layer):
    rows = lambda name: w_in_t[layer, _SEG_START[name]:_SEG_START[name] + _SEG_SIZE[name], :]
    tail_used = sum(_SEG_SIZE[n] for n in _TAIL_ORDER)
    return jnp.concatenate([rows(n) for n in _TAIL_ORDER]
                           + [jnp.zeros((TAIL_W - tail_used, w_in_t.shape[2]), w_in_t.dtype)], axis=0)


def _tile(n, pref):
    t = min(n, pref)
    assert n % t == 0, (n, pref)
    return t


def _layer(x, layer, rope_a, rope_i, norm_mix, norm_ffn, w_in_t, att_q_norm, att_k_norm, mlstm_conv,
           mlstm_gate_bias, mlstm_out_norm, ret_out_norm, w_branch_att, w_branch_mlstm, w_branch_ret,
           w_out, w_ffn_in, w_ffn_out):
    s_len = x.shape[0]
    tm = _tile(s_len, 1024)
    h = rms_norm_bf16(x, norm_mix, tm=_tile(s_len, 512))
    proj = projection_main(h, w_in_t, layer, tm=_tile(s_len, 2048))
    tail = projection_tail(h, _tail_weight(w_in_t, layer), tm=_tile(s_len, 2048))

    def tseg(name):
        return tail[:, _TAIL_START[name]:_TAIL_START[name] + _SEG_SIZE[name]]

    tq = _tile(s_len, 256)
    qt, kk, vt, iqt, ik, wt = attention_prep(proj, tail, rope_a, rope_i, att_q_norm, att_k_norm, tm=tq)
    bias = select_bias(ik, iqt, wt, tq=tq, tk=_tile(s_len, 128))
    y_att = masked_attention(qt, kk, vt, bias, tq=tq, tk=_tile(s_len, 512))

    i_pre = tseg("mi") + mlstm_gate_bias[:MLSTM_HEADS]
    f_pre = tseg("mf") + mlstm_gate_bias[MLSTM_HEADS:]
    n_chunks = s_len // MLSTM_CHUNK
    logf = jax.nn.log_sigmoid(f_pre).reshape(n_chunks, MLSTM_CHUNK, MLSTM_HEADS)
    b_cum = jnp.cumsum(logf, axis=1).reshape(s_len, MLSTM_HEADS)
    gcol = jnp.concatenate([b_cum, i_pre], axis=-1)
    grow = gcol.reshape(n_chunks, MLSTM_CHUNK, 2 * MLSTM_HEADS).transpose(0, 2, 1)
    y_mlstm = mlstm_mixer(proj, gcol, grow, mlstm_out_norm.reshape(-1), mlstm_conv)

    y_ret = retention_mixer(proj, rope_a[0], rope_a[1], ret_out_norm.reshape(-1))

    merged = gated_merge(y_att, y_mlstm, y_ret, w_branch_att, w_branch_mlstm, w_branch_ret, layer, proj,
                         tm=tm, tn=512)
    x = residual_matmul(x, merged, w_out, layer, tm=_tile(s_len, 2048), tn=512)

    h_ffn = rms_norm_bf16(x, norm_ffn, tm=_tile(s_len, 512))
    act = swiglu(h_ffn, w_ffn_in, layer, tm=_tile(s_len, 2048), tn=512)
    x = residual_matmul(x, act, w_ffn_out, layer, tm=_tile(s_len, 2048), tn=256, single_buffer_a=True)
    return x


def kernel(x, norm_mix, norm_ffn, w_in, att_q_norm, att_k_norm, mlstm_conv, mlstm_gate_bias,
           mlstm_out_norm, ret_out_norm, w_branch_att, w_branch_mlstm, w_branch_ret, w_out,
           w_ffn_in, w_ffn_out):
    b, s_len, d = x.shape
    assert d == D_MODEL and s_len % 256 == 0
    rope_a = _rope_tables(s_len, ATT_HEAD_DIM)
    rope_i = _rope_tables(s_len, IDX_DIM)
    w_in_t = jnp.swapaxes(w_in, 1, 2)
    outs = []
    for bi in range(b):
        xb = x[bi]
        for layer in range(w_in.shape[0]):
            xb = _layer(xb, layer, rope_a, rope_i, norm_mix[layer], norm_ffn[layer], w_in_t, att_q_norm[layer],
                        att_k_norm[layer], mlstm_conv[layer], mlstm_gate_bias[layer],
                        mlstm_out_norm[layer], ret_out_norm[layer], w_branch_att, w_branch_mlstm,
                        w_branch_ret, w_out, w_ffn_in, w_ffn_out)
        outs.append(xb)
    return jnp.stack(outs, axis=0)
```

```python
import functools
import math

import jax
import jax.numpy as jnp
import numpy as np
from jax import lax
from jax.experimental import pallas as pl
from jax.experimental.pallas import tpu as pltpu

F32 = jnp.float32
BF16 = jnp.bfloat16

D_MODEL = 2048
ATT_HEADS = 8
ATT_HEAD_DIM = 128
IDX_HEADS = 8
IDX_DIM = 64
INDEX_TOPK = 256
MLSTM_HEADS = 4
MLSTM_QK_DIM = 128
MLSTM_V_DIM = 256
MLSTM_CHUNK = 512
CONV_WIDTH = 4
RET_HEADS = 8
RET_QK_DIM = 128
RET_V_DIM = 128
RET_CHUNK = 256
ROPE_THETA = 10000.0
NORM_EPS = 1e-6
FFN_DIM = 5632
N_BRANCHES = 3

ATT_W = ATT_HEADS * ATT_HEAD_DIM
MLSTM_QK_W = MLSTM_HEADS * MLSTM_QK_DIM
MLSTM_V_W = MLSTM_HEADS * MLSTM_V_DIM
RET_W = RET_HEADS * RET_V_DIM

_SEGMENTS = (
    ("aq", ATT_W), ("ak", ATT_W), ("av", ATT_W), ("iq", IDX_HEADS * IDX_DIM), ("ik", IDX_DIM),
    ("iw", IDX_HEADS), ("mq", MLSTM_QK_W), ("mk", MLSTM_QK_W), ("mv", MLSTM_V_W),
    ("mi", MLSTM_HEADS), ("mf", MLSTM_HEADS), ("mo", MLSTM_V_W), ("rq", RET_W), ("rk", RET_W),
    ("rv", RET_W), ("rg", RET_W), ("gates", N_BRANCHES * D_MODEL),
)
_SEG_SIZE = dict(_SEGMENTS)
_SEG_START = {}
_off = 0
for _name, _size in _SEGMENTS:
    _SEG_START[_name] = _off
    _off += _size

_MAIN_ORDER = ("aq", "ak", "av", "mv", "mo", "rq", "rk", "rv", "rg", "gates", "iq", "mq", "mk")
_TAIL_ORDER = ("ik", "iw", "mi", "mf")
_MAIN_START = {}
_off = 0
for _name in _MAIN_ORDER:
    _MAIN_START[_name] = _off
    _off += _SEG_SIZE[_name]
MAIN_W = _off
_TAIL_START = {}
_off = 0
for _name in _TAIL_ORDER:
    _TAIL_START[_name] = _off
    _off += _SEG_SIZE[_name]
LANES = 128
TAIL_W = LANES

VMEM_LIMIT = 56 * 1024 * 1024
MASK_VALUE = -1e30


def _cparams(*semantics):
    return pltpu.CompilerParams(dimension_semantics=semantics, vmem_limit_bytes=VMEM_LIMIT)


def _rms_rows(x, g):
    ms = jnp.mean(x * x, axis=-1, keepdims=True)
    return x * lax.rsqrt(ms + NORM_EPS) * g


def _rms_norm_kernel(x_ref, g_ref, o_ref):
    o_ref[...] = _rms_rows(x_ref[...], g_ref[...]).astype(o_ref.dtype)


def rms_norm_bf16(x, g, *, tm):
    m, k = x.shape
    return pl.pallas_call(
        _rms_norm_kernel,
        out_shape=jax.ShapeDtypeStruct((m, k), BF16),
        grid=(m // tm,),
        in_specs=[pl.BlockSpec((tm, k), lambda i: (i, 0)), pl.BlockSpec((1, k), lambda i: (0, 0))],
        out_specs=pl.BlockSpec((tm, k), lambda i: (i, 0)),
        compiler_params=_cparams("parallel"),
        name="rms_norm",
    )(x, g.reshape(1, k))


PROJ_TN = 512


def _matmul_nt_kernel(*refs):
    a_ref, wt_ref, o_ref = refs[-3:]
    wt = wt_ref[...].reshape(wt_ref.shape[-2:])
    o_ref[...] = lax.dot_general(a_ref[...], wt.astype(BF16), (((1,), (1,)), ((), ())),
                                 preferred_element_type=F32)


def projection_main(h, w_in_t, layer, *, tm):
    m, k = h.shape
    rows = []
    for name in _MAIN_ORDER:
        assert _SEG_SIZE[name] % PROJ_TN == 0 and _SEG_START[name] % 8 == 0
        rows += [_SEG_START[name] + c for c in range(0, _SEG_SIZE[name], PROJ_TN)]
    src = jnp.asarray([r // 8 for r in rows], jnp.int32)
    grid_spec = pltpu.PrefetchScalarGridSpec(
        num_scalar_prefetch=1,
        grid=(m // tm, len(rows)),
        in_specs=[
            pl.BlockSpec((tm, k), lambda i, j, src: (i, 0)),
            pl.BlockSpec((pl.Element(1), pl.Element(PROJ_TN), pl.Element(k)),
                         lambda i, j, src: (layer, src[j] * 8, 0)),
        ],
        out_specs=pl.BlockSpec((tm, PROJ_TN), lambda i, j, src: (i, j)),
    )
    return pl.pallas_call(
        _matmul_nt_kernel,
        out_shape=jax.ShapeDtypeStruct((m, MAIN_W), F32),
        grid_spec=grid_spec,
        compiler_params=_cparams("parallel", "arbitrary"),
        name="projection_main",
    )(src, h, w_in_t)


def projection_tail(h, w_tail_t, *, tm):
    m, k = h.shape
    return pl.pallas_call(
        _matmul_nt_kernel,
        out_shape=jax.ShapeDtypeStruct((m, TAIL_W), F32),
        grid=(m // tm,),
        in_specs=[pl.BlockSpec((tm, k), lambda i: (i, 0)), pl.BlockSpec((TAIL_W, k), lambda i: (0, 0))],
        out_specs=pl.BlockSpec((tm, TAIL_W), lambda i: (i, 0)),
        compiler_params=_cparams("parallel"),
        name="projection_tail",
    )(h, w_tail_t)


def _swiglu_kernel(h_ref, wg_ref, wu_ref, o_ref):
    h = h_ref[...]
    gate = jnp.dot(h, wg_ref[...].astype(BF16), preferred_element_type=F32)
    up = jnp.dot(h, wu_ref[...].astype(BF16), preferred_element_type=F32)
    o_ref[...] = (gate * jax.nn.sigmoid(gate) * up).astype(o_ref.dtype)


def swiglu(h, w_in_stack, layer, *, tm, tn):
    m, k = h.shape
    f = w_in_stack.shape[2] // 2
    nj = f // tn
    w_in = w_in_stack
    return pl.pallas_call(
        _swiglu_kernel,
        out_shape=jax.ShapeDtypeStruct((m, f), BF16),
        grid=(m // tm, nj),
        in_specs=[
            pl.BlockSpec((tm, k), lambda i, j: (i, 0)),
            _layer_weight_spec(layer, k, tn),
            _layer_weight_spec(layer, k, tn, col_offset_blocks=nj),
        ],
        out_specs=pl.BlockSpec((tm, tn), lambda i, j: (i, j)),
        compiler_params=_cparams("parallel", "arbitrary"),
        name="swiglu",
    )(h, w_in, w_in)


def _residual_matmul_kernel(r_ref, a_ref, w_ref, o_ref):
    o_ref[...] = r_ref[...] + jnp.dot(a_ref[...], w_ref[...].astype(BF16), preferred_element_type=F32)


def _layer_weight_spec(layer, k, tn, col_offset_blocks=0):
    return pl.BlockSpec((None, k, tn), lambda i, j: (layer, 0, j + col_offset_blocks))


def residual_matmul(res, a, w_stack, layer, *, tm, tn, single_buffer_a=False):
    m, k = a.shape
    n = w_stack.shape[2]
    a_mode = dict(pipeline_mode=pl.Buffered(1)) if single_buffer_a else {}
    return pl.pallas_call(
        _residual_matmul_kernel,
        out_shape=jax.ShapeDtypeStruct((m, n), F32),
        grid=(m // tm, n // tn),
        in_specs=[
            pl.BlockSpec((tm, tn), lambda i, j: (i, j)),
            pl.BlockSpec((tm, k), lambda i, j: (i, 0), **a_mode),
            _layer_weight_spec(layer, k, tn),
        ],
        out_specs=pl.BlockSpec((tm, tn), lambda i, j: (i, j)),
        compiler_params=_cparams("parallel", "arbitrary"),
        name="residual_matmul",
    )(res, a, w_stack)


def _merge_kernel(ya_ref, ym_ref, yr_ref, wa_ref, wm_ref, wr_ref, ga_ref, gm_ref, gr_ref, o_ref):
    def branch(y_ref, w_ref, g_ref):
        proj = jnp.dot(y_ref[...], w_ref[...].astype(BF16), preferred_element_type=F32)
        return jax.nn.sigmoid(g_ref[...]) * proj

    merged = branch(ya_ref, wa_ref, ga_ref) + branch(ym_ref, wm_ref, gm_ref) + branch(yr_ref, wr_ref, gr_ref)
    o_ref[...] = merged.astype(o_ref.dtype)


def gated_merge(y_att, y_mlstm, y_ret, w_att, w_mlstm, w_ret, layer, proj, *, tm, tn):
    m, k = y_att.shape
    n = w_att.shape[2]
    gate_blk = _MAIN_START["gates"] // tn
    per_branch = n // tn
    y_spec = pl.BlockSpec((tm, k), lambda i, j: (i, 0), pipeline_mode=pl.Buffered(1))
    w_spec = _layer_weight_spec(layer, k, tn)

    def g_spec(b):
        return pl.BlockSpec((tm, tn), lambda i, j: (i, gate_blk + b * per_branch + j))

    return pl.pallas_call(
        _merge_kernel,
        out_shape=jax.ShapeDtypeStruct((m, n), BF16),
        grid=(m // tm, n // tn),
        in_specs=[y_spec, y_spec, y_spec, w_spec, w_spec, w_spec, g_spec(0), g_spec(1), g_spec(2)],
        out_specs=pl.BlockSpec((tm, tn), lambda i, j: (i, j)),
        compiler_params=_cparams("parallel", "arbitrary"),
        name="gated_merge",
    )(y_att, y_mlstm, y_ret, w_att, w_mlstm, w_ret, proj, proj, proj)


INT_MIN = -(2 ** 31)


def _sortable(x):
    bits = pltpu.bitcast(x, jnp.int32)
    bits = jnp.where(bits == INT_MIN, 0, bits)
    return bits ^ ((bits >> 31) & 0x7FFFFFFF)


GROUP = 256


def _bit_transpose32(words):
    a = list(words)
    j, m = 16, 0x0000FFFF
    while j:
        k = 0
        while k < 32:
            t = (a[k] ^ lax.shift_right_logical(a[k + j], jnp.int32(j))) & m
            a[k] = a[k] ^ t
            a[k + j] = a[k + j] ^ (t << j)
            k = (k + j + 1) & ~j
        j >>= 1
        m = (m ^ (m << j)) & 0xFFFFFFFF
    return a


def _select_kernel(ik_ref, iqt_ref, wt_ref, bias_ref, key_ref, plane_ref, alive_ref, *, tq, tk, topk):
    s_len = ik_ref.shape[0]
    i = pl.program_id(0)
    q_lo = i * tq
    n_groups = (q_lo + tq + GROUP - 1) // GROUP
    n_chunks = n_groups * (GROUP // tk)
    q_pos = q_lo + lax.broadcasted_iota(jnp.int32, (tk, tq), 1)
    row_iota = lax.broadcasted_iota(jnp.int32, (tk, tq), 0)

    @pl.when(i == 0)
    def _():
        plane_ref[...] = jnp.zeros(plane_ref.shape, jnp.int32)

    def score_group(g, carry, *, diagonal):
        g0 = pl.multiple_of(g * GROUP, GROUP)
        slabs = []
        for sub in range(GROUP // tk):
            k0 = g0 + sub * tk
            ik = ik_ref[pl.ds(k0, tk), :]
            acc = jnp.zeros((tk, tq), F32)
            for h in range(IDX_HEADS):
                lg = jnp.dot(ik, iqt_ref[h * IDX_DIM:(h + 1) * IDX_DIM, :], preferred_element_type=F32)
                acc = acc + wt_ref[h:h + 1, :] * jnp.maximum(lg, 0.0)
            key = _sortable(acc)
            if diagonal:
                key = jnp.where(k0 + row_iota <= q_pos, key, INT_MIN)
            key_ref[pl.ds(k0, tk), :] = key
            slabs += [key[8 * j:8 * j + 8, :] for j in range(tk // 8)]
        planes = _bit_transpose32(slabs)
        r0 = pl.multiple_of(g * 8, 8)
        for p in range(32):
            word = planes[31 - p]
            plane_ref[p, pl.ds(r0, 8), :] = ~word if p == 31 else word
        return carry

    assert tq <= GROUP
    lax.fori_loop(0, n_groups - 1, functools.partial(score_group, diagonal=False), 0)
    score_group(n_groups - 1, 0, diagonal=True)

    grp = lax.broadcasted_iota(jnp.int32, alive_ref.shape, 0) >> 3
    alive_ref[...] = jnp.where(grp < n_groups, -1, 0)
    n_gt = jnp.zeros((1, tq), jnp.int32)
    thr_u = jnp.zeros((1, tq), jnp.int32)
    for p in range(31, -1, -1):
        alive = alive_ref[...]
        ones = alive & plane_ref[p]
        c = jnp.sum(lax.population_count(ones), axis=0, keepdims=True)
        take1 = (n_gt + c) >= topk
        alive_ref[...] = jnp.where(take1, ones, alive ^ ones)
        n_gt = n_gt + jnp.where(take1, 0, c)
        thr_u = thr_u | jnp.where(take1, INT_MIN if p == 31 else (1 << p), 0)
    thr = thr_u ^ INT_MIN
    n_eq = jnp.sum(lax.population_count(alive_ref[...]), axis=0, keepdims=True)

    need = topk - n_gt
    excess = jnp.max(n_eq - need)

    def count(pred):
        def body(c, acc):
            k0 = pl.multiple_of(c * tk, tk)
            hit = pred(key_ref[pl.ds(k0, tk), :], k0).astype(jnp.int32)
            return acc + jnp.sum(hit.reshape(tk // 8, 8, tq), axis=0)

        acc = lax.fori_loop(0, n_chunks, body, jnp.zeros((8, tq), jnp.int32))
        return jnp.sum(acc, axis=0, keepdims=True)

    def tie_cut():
        cut = jnp.zeros((1, tq), jnp.int32)
        n_bits = max(1, int(math.ceil(math.log2(s_len))))
        for bit in range(n_bits - 1, -1, -1):
            cand = cut | (1 << bit)
            n_before = count(lambda key, k0, cand=cand: (key == thr) & (k0 + row_iota < cand))
            cut = jnp.where(n_before < need, cand, cut)
        return cut

    cut = lax.cond(excess > 0, tie_cut, lambda: jnp.full((1, tq), s_len, jnp.int32))

    def bias_chunk(c, carry):
        k0 = pl.multiple_of(c * tk, tk)
        key = key_ref[pl.ds(k0, tk), :]
        k_pos = k0 + row_iota
        sel = (key > thr) | ((key == thr) & (k_pos <= cut))
        sel = sel & (k_pos <= q_pos)
        bias_ref[pl.ds(k0, tk), :] = jnp.where(sel, 0.0, MASK_VALUE).astype(bias_ref.dtype)
        return carry

    lax.fori_loop(0, n_chunks, bias_chunk, 0)

    def fill_chunk(c, carry):
        k0 = pl.multiple_of(c * tk, tk)
        bias_ref[pl.ds(k0, tk), :] = jnp.full((tk, tq), MASK_VALUE, bias_ref.dtype)
        return carry

    lax.fori_loop(n_chunks, s_len // tk, fill_chunk, 0)


def select_bias(ik, iqt, wt, *, tq, tk):
    s_len = ik.shape[0]
    topk = min(INDEX_TOPK, s_len // 4)
    kern = functools.partial(_select_kernel, tq=tq, tk=tk, topk=topk)
    return pl.pallas_call(
        kern,
        out_shape=jax.ShapeDtypeStruct((s_len, s_len), BF16),
        grid=(s_len // tq,),
        in_specs=[
            pl.BlockSpec((s_len, IDX_DIM), lambda i: (0, 0)),
            pl.BlockSpec((IDX_HEADS * IDX_DIM, tq), lambda i: (0, i)),
            pl.BlockSpec((IDX_HEADS, tq), lambda i: (0, i)),
        ],
        out_specs=pl.BlockSpec((s_len, tq), lambda i: (0, i)),
        scratch_shapes=[
            pltpu.VMEM((s_len, tq), jnp.int32),
            pltpu.VMEM((32, s_len // 32, tq), jnp.int32),
            pltpu.VMEM((s_len // 32, tq), jnp.int32),
        ],
        compiler_params=_cparams("arbitrary"),
        name="select_bias",
    )(ik, iqt, wt)


ATT_Q_SCALE = ATT_HEAD_DIM ** -0.5 * math.log2(math.e)


VT_ONES_ROWS = 16
VT_ROWS = ATT_HEAD_DIM + VT_ONES_ROWS


def _attention_kernel(qi_ref, ki_ref, qt_ref, k_ref, vt_ref, bias_ref, o_ref, m_ref, acc_ref,
                      *s_refs, tq, tk):
    step = pl.program_id(0)
    i = qi_ref[step]
    kb = ki_ref[step]
    last_kb = (i * tq + tq - 1) // tk
    dh = ATT_HEAD_DIM

    @pl.when(kb == 0)
    def _():
        m_ref[...] = jnp.full(m_ref.shape, MASK_VALUE, F32)
        acc_ref[...] = jnp.zeros(acc_ref.shape, F32)

    bias = bias_ref[...].astype(F32)

    m_new = []
    for h in range(ATT_HEADS):
        hs = slice(h * dh, (h + 1) * dh)
        s = jnp.dot(k_ref[:, hs], qt_ref[hs, :], preferred_element_type=F32) + bias
        s_refs[h][0] = s
        m_new.append(jnp.maximum(m_ref[h], jnp.max(s, axis=0, keepdims=True)))
    zero = lax.shift_right_logical(kb, 30)
    for h in range(ATT_HEADS):
        alpha = jnp.exp2(m_ref[h] - m_new[h])
        p = jnp.exp2(s_refs[h][zero] - m_new[h])
        acc_ref[h] = alpha * acc_ref[h] + jnp.dot(vt_ref[h * VT_ROWS:(h + 1) * VT_ROWS, :],
                                                  p.astype(BF16), preferred_element_type=F32)
        m_ref[h] = m_new[h]

    @pl.when(kb == last_kb)
    def _():
        for h in range(ATT_HEADS):
            out_t = acc_ref[h, :dh, :] / acc_ref[h, dh:dh + 1, :]
            o_ref[:, h * dh:(h + 1) * dh] = out_t.T.astype(o_ref.dtype)


def masked_attention(qt, k, vt, bias, *, tq, tk):
    width, s_len = qt.shape
    pairs = [(i, kb) for i in range(s_len // tq) for kb in range((i * tq + tq - 1) // tk + 1)]
    qi = jnp.asarray([p[0] for p in pairs], jnp.int32)
    ki = jnp.asarray([p[1] for p in pairs], jnp.int32)
    grid_spec = pltpu.PrefetchScalarGridSpec(
        num_scalar_prefetch=2,
        grid=(len(pairs),),
        in_specs=[
            pl.BlockSpec((width, tq), lambda s, qi, ki: (0, qi[s])),
            pl.BlockSpec((tk, width), lambda s, qi, ki: (ki[s], 0)),
            pl.BlockSpec((ATT_HEADS * VT_ROWS, tk), lambda s, qi, ki: (0, ki[s])),
            pl.BlockSpec((tk, tq), lambda s, qi, ki: (ki[s], qi[s])),
        ],
        out_specs=pl.BlockSpec((tq, width), lambda s, qi, ki: (qi[s], 0)),
        scratch_shapes=[
            pltpu.VMEM((ATT_HEADS, 1, tq), F32),
            pltpu.VMEM((ATT_HEADS, VT_ROWS, tq), F32),
        ] + [pltpu.VMEM((1, tk, tq), F32) for _ in range(ATT_HEADS)],
    )
    return pl.pallas_call(
        functools.partial(_attention_kernel, tq=tq, tk=tk),
        out_shape=jax.ShapeDtypeStruct((s_len, width), BF16),
        grid_spec=grid_spec,
        compiler_params=_cparams("arbitrary"),
        name="masked_attention",
    )(qi, ki, qt, k, vt, bias)


HALO = 8


def _mlstm_kernel(q_ref, k_ref, v_ref, o_ref, gcol_ref, grow_ref, gain_ref, cw_ref, y_ref,
                  c_ref, n_ref, m_ref, halo_ref, qk_ref, qc_ref):
    L = q_ref.shape[0]
    dk, dv, nh = MLSTM_QK_DIM, MLSTM_V_DIM, MLSTM_HEADS
    qk_w = nh * dk

    @pl.when(pl.program_id(0) == 0)
    def _():
        c_ref[...] = jnp.zeros(c_ref.shape, F32)
        n_ref[...] = jnp.zeros(n_ref.shape, F32)
        m_ref[...] = jnp.zeros(m_ref.shape, F32)
        halo_ref[...] = jnp.zeros(halo_ref.shape, F32)

    def causal_conv(x_ref, slot, taps):
        x = x_ref[...]
        ext = jnp.concatenate([halo_ref[slot], x], axis=0)
        out = x * taps[CONV_WIDTH - 1:CONV_WIDTH, :]
        for j in range(CONV_WIDTH - 1):
            lo = HALO - (CONV_WIDTH - 1 - j)
            out = out + ext[lo:lo + L, :] * taps[j:j + 1, :]
        halo_ref[slot] = x[L - HALO:, :]
        return out

    q_all = causal_conv(q_ref, 0, cw_ref[:, :qk_w])
    k_all = causal_conv(k_ref, 1, cw_ref[:, qk_w:])

    tril = lax.broadcasted_iota(jnp.int32, (L, L), 1) <= lax.broadcasted_iota(jnp.int32, (L, L), 0)
    gcol = gcol_ref[...]
    grow = grow_ref[0]
    stats = []
    for h in range(nh):
        q = q_all[:, h * dk:(h + 1) * dk]
        k = k_all[:, h * dk:(h + 1) * dk] * (dk ** -0.5)
        b_col = gcol[:, h:h + 1]
        i_col = gcol[:, nh + h:nh + h + 1]
        b_row = grow[h:h + 1, :]
        i_row = grow[nh + h:nh + h + 1, :]
        m_prev = m_ref[h]
        c_prev = c_ref[h]
        n_prev = n_ref[h]
        qb = q.astype(BF16)
        vb = v_ref[:, h * dv:(h + 1) * dv].astype(BF16)
        qk_ref[h] = lax.dot_general(qb, k.astype(BF16), (((1,), (1,)), ((), ())),
                                    preferred_element_type=F32)
        qc_ref[h] = jnp.dot(qb, c_prev.astype(BF16), preferred_element_type=F32)
        qn = jnp.sum(q * n_prev, axis=-1, keepdims=True)

        b_end = b_row[:, L - 1:L]
        end_row = b_end - b_row + i_row
        m_new = jnp.maximum(b_end + m_prev, jnp.max(end_row, axis=-1, keepdims=True))
        decay = jnp.exp(b_end + m_prev - m_new)
        end_col = b_end - b_col + i_col
        ka = k * jnp.exp(end_col - m_new)
        c_ref[h] = decay * c_prev + jnp.dot(ka.T.astype(BF16), vb, preferred_element_type=F32)
        n_ref[h] = decay * n_prev + jnp.sum(ka, axis=0, keepdims=True)
        m_ref[h] = m_new
        stats.append((b_col, b_row, i_row, m_prev, qn))

    for h in range(nh):
        b_col, b_row, i_row, m_prev, qn = stats[h]
        vb = v_ref[:, h * dv:(h + 1) * dv].astype(BF16)
        log_intra = jnp.where(tril, b_col - b_row + i_row, -jnp.inf)
        m_inter = b_col + m_prev
        m_t = jnp.maximum(m_inter, jnp.max(log_intra, axis=-1, keepdims=True))
        w_inter = jnp.exp(m_inter - m_t)
        p = jnp.exp(log_intra - m_t) * qk_ref[h]
        num = w_inter * qc_ref[h] + jnp.dot(p.astype(BF16), vb, preferred_element_type=F32)
        den = w_inter * qn + jnp.sum(p, axis=-1, keepdims=True)
        h_out = num / jnp.maximum(jnp.abs(den), jnp.exp(-m_t))

        gain = gain_ref[:, h * dv:(h + 1) * dv]
        normed = h_out * lax.rsqrt(jnp.mean(h_out * h_out, axis=-1, keepdims=True) + NORM_EPS) * gain
        y_ref[:, h * dv:(h + 1) * dv] = (normed * jax.nn.sigmoid(o_ref[:, h * dv:(h + 1) * dv])).astype(y_ref.dtype)


def mlstm_mixer(proj, gcol, grow, gain, conv_w):
    s_len = proj.shape[0]
    L = MLSTM_CHUNK
    q_blk = _MAIN_START["mq"] // MLSTM_QK_W
    k_blk = _MAIN_START["mk"] // MLSTM_QK_W
    v_blk = _MAIN_START["mv"] // MLSTM_V_W
    o_blk = _MAIN_START["mo"] // MLSTM_V_W
    return pl.pallas_call(
        _mlstm_kernel,
        out_shape=jax.ShapeDtypeStruct((s_len, MLSTM_V_W), BF16),
        grid=(s_len // L,),
        in_specs=[
            pl.BlockSpec((L, MLSTM_QK_W), lambda c: (c, q_blk)),
            pl.BlockSpec((L, MLSTM_QK_W), lambda c: (c, k_blk)),
            pl.BlockSpec((L, MLSTM_V_W), lambda c: (c, v_blk)),
            pl.BlockSpec((L, MLSTM_V_W), lambda c: (c, o_blk)),
            pl.BlockSpec((L, 2 * MLSTM_HEADS), lambda c: (c, 0)),
            pl.BlockSpec((1, 2 * MLSTM_HEADS, L), lambda c: (c, 0, 0)),
            pl.BlockSpec((1, MLSTM_V_W), lambda c: (0, 0)),
            pl.BlockSpec((CONV_WIDTH, 2 * MLSTM_QK_W), lambda c: (0, 0)),
        ],
        out_specs=pl.BlockSpec((L, MLSTM_V_W), lambda c: (c, 0)),
        scratch_shapes=[
            pltpu.VMEM((MLSTM_HEADS, MLSTM_QK_DIM, MLSTM_V_DIM), F32),
            pltpu.VMEM((MLSTM_HEADS, 1, MLSTM_QK_DIM), F32),
            pltpu.VMEM((MLSTM_HEADS, 1, 1), F32),
            pltpu.VMEM((2, HALO, MLSTM_QK_W), F32),
            pltpu.VMEM((MLSTM_HEADS, L, L), F32),
            pltpu.VMEM((MLSTM_HEADS, L, MLSTM_V_DIM), F32),
        ],
        compiler_params=_cparams("arbitrary"),
        name="mlstm_mixer",
    )(proj, proj, proj, proj, gcol, grow, gain.reshape(1, MLSTM_V_W), conv_w)


def _rope_full(x, cos2, sin2):
    return x * cos2 + pltpu.roll(x, x.shape[-1] // 2, 1) * sin2


def _retention_kernel(q_ref, k_ref, v_ref, g_ref, cos_ref, sin_ref, dint_ref, qdec_ref, kdec_ref,
                      cdec_ref, gain_ref, y_ref, r_ref, qk_ref, cross_ref):
    dk, dv = RET_QK_DIM, RET_V_DIM

    @pl.when(pl.program_id(0) == 0)
    def _():
        r_ref[...] = jnp.zeros(r_ref.shape, F32)

    cos2 = cos_ref[...]
    sin2 = sin_ref[...]
    for h in range(RET_HEADS):
        q = _rope_full(q_ref[:, h * dk:(h + 1) * dk], cos2, sin2)
        k = _rope_full(k_ref[:, h * dk:(h + 1) * dk], cos2, sin2) * (dk ** -0.5)
        vb = v_ref[:, h * dv:(h + 1) * dv].astype(BF16)
        r_prev = r_ref[h]
        qk_ref[h] = lax.dot_general(q.astype(BF16), k.astype(BF16), (((1,), (1,)), ((), ())),
                                    preferred_element_type=F32)
        cross_ref[h] = jnp.dot((q * qdec_ref[h]).astype(BF16), r_prev.astype(BF16),
                               preferred_element_type=F32)
        kd = k * kdec_ref[h]
        r_ref[h] = cdec_ref[h] * r_prev + jnp.dot(kd.T.astype(BF16), vb, preferred_element_type=F32)

    for h in range(RET_HEADS):
        vb = v_ref[:, h * dv:(h + 1) * dv].astype(BF16)
        inner = jnp.dot((qk_ref[h] * dint_ref[h]).astype(BF16), vb, preferred_element_type=F32)
        out = inner + cross_ref[h]

        mu = jnp.mean(out, axis=-1, keepdims=True)
        xc = out - mu
        normed = xc * lax.rsqrt(jnp.mean(xc * xc, axis=-1, keepdims=True) + NORM_EPS)
        gate = g_ref[:, h * dv:(h + 1) * dv]
        y = normed * gain_ref[:, h * dv:(h + 1) * dv] * (gate * jax.nn.sigmoid(gate))
        y_ref[:, h * dv:(h + 1) * dv] = y.astype(y_ref.dtype)


def _retention_tables(size):
    log_gamma = jnp.log(1.0 - 2.0 ** (-5.0 - jnp.arange(RET_HEADS, dtype=F32)))
    pos = jnp.arange(size, dtype=F32)
    rel = pos[:, None] - pos[None, :]
    decay_intra = jnp.where(rel >= 0, jnp.exp(log_gamma[:, None, None] * jnp.maximum(rel, 0.0)), 0.0)
    q_decay = jnp.exp(log_gamma[:, None] * (pos + 1.0))[:, :, None]
    k_decay = jnp.exp(log_gamma[:, None] * (size - 1.0 - pos))[:, :, None]
    chunk_decay = jnp.exp(log_gamma * size)[:, None, None]
    return decay_intra, q_decay, k_decay, chunk_decay


def retention_mixer(proj, cos2, sin2, gain):
    s_len = proj.shape[0]
    L = RET_CHUNK
    q_blk = _MAIN_START["rq"] // RET_W
    k_blk = _MAIN_START["rk"] // RET_W
    v_blk = _MAIN_START["rv"] // RET_W
    g_blk = _MAIN_START["rg"] // RET_W
    decay_intra, q_decay, k_decay, chunk_decay = _retention_tables(L)
    nh = RET_HEADS
    return pl.pallas_call(
        _retention_kernel,
        out_shape=jax.ShapeDtypeStruct((s_len, RET_W), BF16),
        grid=(s_len // L,),
        in_specs=[
            pl.BlockSpec((L, RET_W), lambda c: (c, q_blk)),
            pl.BlockSpec((L, RET_W), lambda c: (c, k_blk)),
            pl.BlockSpec((L, RET_W), lambda c: (c, v_blk)),
            pl.BlockSpec((L, RET_W), lambda c: (c, g_blk)),
            pl.BlockSpec((L, RET_QK_DIM), lambda c: (c, 0)),
            pl.BlockSpec((L, RET_QK_DIM), lambda c: (c, 0)),
            pl.BlockSpec((nh, L, L), lambda c: (0, 0, 0)),
            pl.BlockSpec((nh, L, 1), lambda c: (0, 0, 0)),
            pl.BlockSpec((nh, L, 1), lambda c: (0, 0, 0)),
            pl.BlockSpec((nh, 1, 1), lambda c: (0, 0, 0)),
            pl.BlockSpec((1, RET_W), lambda c: (0, 0)),
        ],
        out_specs=pl.BlockSpec((L, RET_W), lambda c: (c, 0)),
        scratch_shapes=[
            pltpu.VMEM((nh, RET_QK_DIM, RET_V_DIM), F32),
            pltpu.VMEM((nh, L, L), F32),
            pltpu.VMEM((nh, L, RET_V_DIM), F32),
        ],
        compiler_params=_cparams("arbitrary"),
        name="retention_mixer",
    )(proj, proj, proj, proj, cos2, sin2, decay_intra, q_decay, k_decay, chunk_decay,
      gain.reshape(1, RET_W))


def _rope_tables(s_len, d):
    half = d // 2
    inv_freq = ROPE_THETA ** (-jnp.arange(half, dtype=F32) * 2.0 / d)
    ang = jnp.arange(s_len, dtype=jnp.int32).astype(F32)[:, None] * inv_freq[None, :]
    cos, sin = jnp.cos(ang), jnp.sin(ang)
    reps = LANES // d
    return jnp.tile(jnp.concatenate([cos, cos], -1), (1, reps)), jnp.tile(jnp.concatenate([-sin, sin], -1), (1, reps))


def _rope_pair(x, cos4, sin4):
    half = IDX_DIM // 2
    lane = lax.broadcasted_iota(jnp.int32, x.shape, 1)
    partner = jnp.where((lane % IDX_DIM) < half, pltpu.roll(x, LANES - half, 1), pltpu.roll(x, half, 1))
    return x * cos4 + partner * sin4


def _att_prep_kernel(aq_ref, ak_ref, av_ref, iq_ref, tail_ref, cosa_ref, sina_ref, cosi_ref, sini_ref,
                     gq_ref, gk_ref, qt_ref, k_ref, vt_ref, iqt_ref, ik_ref, wt_ref):
    dh = ATT_HEAD_DIM
    cos_a, sin_a = cosa_ref[...], sina_ref[...]
    cos_i, sin_i = cosi_ref[...], sini_ref[...]
    for h in range(ATT_HEADS):
        hs = slice(h * dh, (h + 1) * dh)
        q = _rope_full(_rms_rows(aq_ref[:, hs], gq_ref[...]), cos_a, sin_a)
        qt_ref[hs, :] = (q * ATT_Q_SCALE).T.astype(qt_ref.dtype)
        k = _rope_full(_rms_rows(ak_ref[:, hs], gk_ref[...]), cos_a, sin_a)
        k_ref[:, hs] = k.astype(k_ref.dtype)
        vt_ref[h * VT_ROWS:h * VT_ROWS + dh, :] = av_ref[:, hs].T.astype(vt_ref.dtype)
        vt_ref[h * VT_ROWS + dh:(h + 1) * VT_ROWS, :] = jnp.ones((VT_ONES_ROWS, vt_ref.shape[1]), vt_ref.dtype)
    for c in range(IDX_HEADS * IDX_DIM // LANES):
        cs = slice(c * LANES, (c + 1) * LANES)
        iqt_ref[cs, :] = _rope_pair(iq_ref[:, cs], cos_i, sin_i).T.astype(iqt_ref.dtype)
    tail = tail_ref[...]
    ik_lo = _TAIL_START["ik"]
    ik_ref[...] = _rope_pair(tail, cos_i, sin_i)[:, ik_lo:ik_lo + IDX_DIM].astype(ik_ref.dtype)
    iw_lo = _TAIL_START["iw"]
    wt_ref[...] = (tail * (IDX_HEADS ** -0.5 * IDX_DIM ** -0.5)).T[iw_lo:iw_lo + IDX_HEADS, :]


def attention_prep(proj, tail, rope_a, rope_i, gq, gk, *, tm):
    s_len = proj.shape[0]
    iq_w = IDX_HEADS * IDX_DIM
    row = lambda width, blk: pl.BlockSpec((tm, width), lambda i: (i, blk))
    col = lambda height: pl.BlockSpec((height, tm), lambda i: (0, i))
    gain = pl.BlockSpec((1, ATT_HEAD_DIM), lambda i: (0, 0))
    return pl.pallas_call(
        _att_prep_kernel,
        out_shape=[
            jax.ShapeDtypeStruct((ATT_W, s_len), BF16),
            jax.ShapeDtypeStruct((s_len, ATT_W), BF16),
            jax.ShapeDtypeStruct((ATT_HEADS * VT_ROWS, s_len), BF16),
            jax.ShapeDtypeStruct((iq_w, s_len), BF16),
            jax.ShapeDtypeStruct((s_len, IDX_DIM), BF16),
            jax.ShapeDtypeStruct((IDX_HEADS, s_len), F32),
        ],
        grid=(s_len // tm,),
        in_specs=[
            row(ATT_W, _MAIN_START["aq"] // ATT_W), row(ATT_W, _MAIN_START["ak"] // ATT_W),
            row(ATT_W, _MAIN_START["av"] // ATT_W), row(iq_w, _MAIN_START["iq"] // iq_w),
            row(TAIL_W, 0), row(LANES, 0), row(LANES, 0), row(LANES, 0), row(LANES, 0), gain, gain,
        ],
        out_specs=[col(ATT_W), row(ATT_W, 0), col(ATT_HEADS * VT_ROWS), col(iq_w), row(IDX_DIM, 0), col(IDX_HEADS)],
        compiler_params=_cparams("parallel"),
        name="attention_prep",
    )(proj, proj, proj, proj, tail, rope_a[0], rope_a[1], rope_i[0], rope_i[1],
      gq.reshape(1, ATT_HEAD_DIM), gk.reshape(1, ATT_HEAD_DIM))


def _tail_weight(w_in_t, layer):
    rows = lambda name: w_in_t[layer, _SEG_START[name]:_SEG_START[name] + _SEG_SIZE[name], :]
    tail_used = sum(_SEG_SIZE[n] for n in _TAIL_ORDER)
    return jnp.concatenate([rows(n) for n in _TAIL_ORDER]
                           + [jnp.zeros((TAIL_W - tail_used, w_in_t.shape[2]), w_in_t.dtype)], axis=0)


def _tile(n, pref):
    t = min(n, pref)
    assert n % t == 0, (n, pref)
    return t


def _layer(x, layer, rope_a, rope_i, norm_mix, norm_ffn, w_in_t, att_q_norm, att_k_norm, mlstm_conv,
           mlstm_gate_bias, mlstm_out_norm, ret_out_norm, w_branch_att, w_branch_mlstm, w_branch_ret,
           w_out, w_ffn_in, w_ffn_out):
    s_len = x.shape[0]
    tm = _tile(s_len, 1024)
    h = rms_norm_bf16(x, norm_mix, tm=_tile(s_len, 512))
    proj = projection_main(h, w_in_t, layer, tm=_tile(s_len, 2048))
    tail = projection_tail(h, _tail_weight(w_in_t, layer), tm=_tile(s_len, 2048))

    def tseg(name):
        return tail[:, _TAIL_START[name]:_TAIL_START[name] + _SEG_SIZE[name]]

    tq = _tile(s_len, 256)
    qt, kk, vt, iqt, ik, wt = attention_prep(proj, tail, rope_a, rope_i, att_q_norm, att_k_norm, tm=tq)
    bias = select_bias(ik, iqt, wt, tq=tq, tk=_tile(s_len, 128))
    y_att = masked_attention(qt, kk, vt, bias, tq=tq, tk=_tile(s_len, 512))

    i_pre = tseg("mi") + mlstm_gate_bias[:MLSTM_HEADS]
    f_pre = tseg("mf") + mlstm_gate_bias[MLSTM_HEADS:]
    n_chunks = s_len // MLSTM_CHUNK
    logf = jax.nn.log_sigmoid(f_pre).reshape(n_chunks, MLSTM_CHUNK, MLSTM_HEADS)
    b_cum = jnp.cumsum(logf, axis=1).reshape(s_len, MLSTM_HEADS)
    gcol = jnp.concatenate([b_cum, i_pre], axis=-1)
    grow = gcol.reshape(n_chunks, MLSTM_CHUNK, 2 * MLSTM_HEADS).transpose(0, 2, 1)
    y_mlstm = mlstm_mixer(proj, gcol, grow, mlstm_out_norm.reshape(-1), mlstm_conv)

    y_ret = retention_mixer(proj, rope_a[0], rope_a[1], ret_out_norm.reshape(-1))

    merged = gated_merge(y_att, y_mlstm, y_ret, w_branch_att, w_branch_mlstm, w_branch_ret, layer, proj,
                         tm=_tile(s_len, 2048), tn=256)
    x = residual_matmul(x, merged, w_out, layer, tm=_tile(s_len, 2048), tn=512)

    h_ffn = rms_norm_bf16(x, norm_ffn, tm=_tile(s_len, 512))
    act = swiglu(h_ffn, w_ffn_in, layer, tm=_tile(s_len, 2048), tn=512)
    x = residual_matmul(x, act, w_ffn_out, layer, tm=_tile(s_len, 2048), tn=256, single_buffer_a=True)
    return x


def kernel(x, norm_mix, norm_ffn, w_in, att_q_norm, att_k_norm, mlstm_conv, mlstm_gate_bias,
           mlstm_out_norm, ret_out_norm, w_branch_att, w_branch_mlstm, w_branch_ret, w_out,
           w_ffn_in, w_ffn_out):
    b, s_len, d = x.shape
    assert d == D_MODEL and s_len % 256 == 0
    rope_a = _rope_tables(s_len, ATT_HEAD_DIM)
    rope_i = _rope_tables(s_len, IDX_DIM)
    w_in_t = jnp.swapaxes(w_in, 1, 2)
    outs = []
    for bi in range(b):
        xb = x[bi]
        for layer in range(w_in.shape[0]):
            xb = _layer(xb, layer, rope_a, rope_i, norm_mix[layer], norm_ffn[layer], w_in_t, att_q_norm[layer],
                        att_k_norm[layer], mlstm_conv[layer], mlstm_gate_bias[layer],
                        mlstm_out_norm[layer], ret_out_norm[layer], w_branch_att, w_branch_mlstm,
                        w_branch_ret, w_out, w_ffn_in, w_ffn_out)
        outs.append(xb)
    return jnp.stack(outs, axis=0)
```
